```python
import math
import jax, jax.numpy as jnp
from jax import lax
import numpy as np

D_MODEL = 1024
BATCH = 8
SEQ = 4096
DEPTH = 1
DEC_BATCH = 128
DEC_SEQ = 8
PAST_LEN = 16384
PAGE_SIZE = 128

N_HEADS = 8
N_KV_HEADS = 2
HEAD_DIM = 64
Q_GROUP = N_HEADS // N_KV_HEADS
WINDOW = 128
ROPE_THETA = 10000.0
ATTN_WIDTH = N_HEADS * HEAD_DIM
KV_WIDTH = N_KV_HEADS * HEAD_DIM
SSM_WIDTH = 512
GROUP_SIZE = 16
N_GROUPS = SSM_WIDTH // GROUP_SIZE
STATE_DIM = 64
DT_MIN = 1e-3
DT_MAX = 1e-1
IN_WIDTH = ATTN_WIDTH + 2 * KV_WIDTH + SSM_WIDTH + 2 * D_MODEL
SPLIT_IDX = (ATTN_WIDTH, ATTN_WIDTH + KV_WIDTH, ATTN_WIDTH + 2 * KV_WIDTH,
             ATTN_WIDTH + 2 * KV_WIDTH + SSM_WIDTH, ATTN_WIDTH + 2 * KV_WIDTH + SSM_WIDTH + D_MODEL)
PEER_HEADS = 8
N_KEYS = 128
N_EXPERTS = N_KEYS * N_KEYS
KEY_DIM = 128
PEER_TOPK = 16
PEER_BLOCK = 128
PLE_DIM = 256
DN_ALPHA = (2.0 * DEPTH) ** 0.25
DN_BETA = (8.0 * DEPTH) ** -0.25
LN_EPS = 1e-5

kernel_name = 'swa_sink_s5_peer_hybrid_step'


def layer_norm(x, g, b):
    xf = x.astype(jnp.float32)
    mu = jnp.mean(xf, -1, keepdims=True)
    var = jnp.mean(jnp.square(xf - mu), -1, keepdims=True)
    return ((xf - mu) * lax.rsqrt(var + LN_EPS) * g.astype(jnp.float32) + b.astype(jnp.float32)).astype(x.dtype)


def rope(x, pos):
    half = HEAD_DIM // 2
    inv = ROPE_THETA ** (-jnp.arange(half, dtype=jnp.float32) / half)
    ang = pos.astype(jnp.float32)[:, None] * inv[None, :]
    cos = jnp.cos(ang)[None, :, None, :]
    sin = jnp.sin(ang)[None, :, None, :]
    xf = x.astype(jnp.float32)
    x1, x2 = xf[..., :half], xf[..., half:]
    return jnp.concatenate([x1 * cos - x2 * sin, x1 * sin + x2 * cos], -1).astype(x.dtype)


def sink_attention(q, k, v, sinks, mask):
    s = jnp.einsum('bnqhgd,bnkhd->bnhgqk', q, k).astype(jnp.float32) * (HEAD_DIM ** -0.5)
    s = jnp.where(mask[None, :, None, None], s, -jnp.inf)
    sink = sinks.astype(jnp.float32).reshape(N_KV_HEADS, Q_GROUP)[None, None, :, :, None, None]
    m = jnp.maximum(jnp.max(s, -1, keepdims=True), sink)
    e = jnp.exp(s - m)
    denom = jnp.sum(e, -1, keepdims=True) + jnp.exp(sink - m)
    p = (e / denom).astype(v.dtype)
    return jnp.einsum('bnhgqk,bnkhd->bnqhgd', p, v)


def attn_prompt(q, k, v, sinks):
    b, l = q.shape[:2]
    nb = l // WINDOW
    qb = q.reshape(b, nb, WINDOW, N_KV_HEADS, Q_GROUP, HEAD_DIM)
    kb = k.reshape(b, nb, WINDOW, N_KV_HEADS, HEAD_DIM)
    vb = v.reshape(b, nb, WINDOW, N_KV_HEADS, HEAD_DIM)
    zk = jnp.zeros_like(kb[:, :1])
    k2 = jnp.concatenate([jnp.concatenate([zk, kb[:, :-1]], 1), kb], 2)
    v2 = jnp.concatenate([jnp.concatenate([zk, vb[:, :-1]], 1), vb], 2)
    qi = jnp.arange(WINDOW)[:, None]
    kj = jnp.arange(2 * WINDOW)[None, :]
    rel = qi + WINDOW - kj
    band = (rel >= 0) & (rel < WINDOW)
    blk = jnp.arange(nb)[:, None, None]
    valid = (blk * WINDOW + kj[None] - WINDOW) >= 0
    mask = band[None] & valid
    o = sink_attention(qb, k2, v2, sinks, mask)
    return o.reshape(b, l, ATTN_WIDTH)


def attn_sample(q, k, v, k_buf, v_buf, sinks):
    b, t = q.shape[:2]
    k2 = jnp.concatenate([k_buf, k], 1)
    v2 = jnp.concatenate([v_buf, v], 1)
    qi = jnp.arange(t)[:, None]
    kj = jnp.arange(WINDOW + t)[None, :]
    rel = qi + WINDOW - kj
    mask = ((rel >= 0) & (rel < WINDOW))[None]
    o = sink_attention(q.reshape(b, 1, t, N_KV_HEADS, Q_GROUP, HEAD_DIM), k2[:, None], v2[:, None], sinks, mask)
    return o.reshape(b, t, ATTN_WIDTH), k2[:, -WINDOW:], v2[:, -WINDOW:]


def ssm_discretize(lam_re, lam_im, log_dt):
    dt = jnp.exp(log_dt.astype(jnp.float32))[:, None]
    lr = lam_re.astype(jnp.float32)
    li = lam_im.astype(jnp.float32)
    mag = jnp.exp(lr * dt)
    a_re = mag * jnp.cos(li * dt)
    a_im = mag * jnp.sin(li * dt)
    nr, ni = a_re - 1.0, a_im
    den = lr * lr + li * li
    c_re = (nr * lr + ni * li) / den
    c_im = (ni * lr - nr * li) / den
    return a_re, a_im, c_re, c_im


def ssm_combine(e1, e2):
    a1r, a1i, b1r, b1i = e1
    a2r, a2i, b2r, b2i = e2
    return (a2r * a1r - a2i * a1i, a2r * a1i + a2i * a1r,
            a2r * b1r - a2i * b1i + b2r, a2r * b1i + a2i * b1r + b2i)


def ssm_branch(u, h0_re, h0_im, lam_re, lam_im, log_dt, b_re, b_im, c_re, c_im, d, w_glu, b_glu):
    bsz, l = u.shape[:2]
    uf = u.astype(jnp.float32).reshape(bsz, l, N_GROUPS, GROUP_SIZE)
    a_re, a_im, k_re, k_im = ssm_discretize(lam_re, lam_im, log_dt)
    bu_re = jnp.einsum('blgc,gpc->blgp', uf, b_re.astype(jnp.float32))
    bu_im = jnp.einsum('blgc,gpc->blgp', uf, b_im.astype(jnp.float32))
    x_re = k_re * bu_re - k_im * bu_im
    x_im = k_re * bu_im + k_im * bu_re
    pr, pi, hr, hi = lax.associative_scan(
        ssm_combine, (jnp.broadcast_to(a_re, x_re.shape), jnp.broadcast_to(a_im, x_re.shape), x_re, x_im), axis=1)
    h0r = h0_re.astype(jnp.float32)[:, None]
    h0i = h0_im.astype(jnp.float32)[:, None]
    hr = hr + pr * h0r - pi * h0i
    hi = hi + pr * h0i + pi * h0r
    y = (jnp.einsum('blgp,gcp->blgc', hr, c_re.astype(jnp.float32))
         - jnp.einsum('blgp,gcp->blgc', hi, c_im.astype(jnp.float32))
         + d.astype(jnp.float32).reshape(N_GROUPS, GROUP_SIZE) * uf)
    y = jax.nn.gelu(y.reshape(bsz, l, SSM_WIDTH), approximate=False)
    out = y * jax.nn.sigmoid(y @ w_glu.astype(jnp.float32) + b_glu.astype(jnp.float32))
    return out.astype(u.dtype), hr[:, -1].astype(u.dtype), hi[:, -1].astype(u.dtype)


def peer(x, w_q, keys1, keys2, u_tab, v_tab):
    shp = x.shape
    xt = x.reshape(-1, D_MODEL)
    t = xt.shape[0]
    nblk = -(-t // PEER_BLOCK)
    xt = jnp.pad(xt, ((0, nblk * PEER_BLOCK - t), (0, 0))).reshape(nblk, PEER_BLOCK, D_MODEL)
    k1 = keys1.astype(jnp.float32)
    k2 = keys2.astype(jnp.float32)

    def one_block(xb):
        q = (xb @ w_q).astype(jnp.float32).reshape(PEER_BLOCK, PEER_HEADS, 2, KEY_DIM)
        s1 = jnp.einsum('thd,hkd->thk', q[:, :, 0], k1)
        s2 = jnp.einsum('thd,hkd->thk', q[:, :, 1], k2)
        v1, i1 = lax.top_k(s1, PEER_TOPK)
        v2, i2 = lax.top_k(s2, PEER_TOPK)
        cand = (v1[..., :, None] + v2[..., None, :]).reshape(PEER_BLOCK, PEER_HEADS, PEER_TOPK * PEER_TOPK)
        sc, ci = lax.top_k(cand, PEER_TOPK)
        e1 = jnp.take_along_axis(i1, ci // PEER_TOPK, -1)
        e2 = jnp.take_along_axis(i2, ci % PEER_TOPK, -1)
        idx = e1 * N_KEYS + e2
        g = jax.nn.softmax(sc, -1)
        ue = u_tab[idx]
        ve = v_tab[idx]
        h = jax.nn.gelu(jnp.einsum('td,thkd->thk', xb, ue).astype(jnp.float32), approximate=False)
        return jnp.einsum('thk,thkd->td', (g * h).astype(ve.dtype), ve).astype(xb.dtype)

    y = lax.map(one_block, xt)
    return y.reshape(-1, D_MODEL)[:t].reshape(shp)


def decoder_layer(x, p, pos, win_k, win_v, h0_re, h0_im,
                  w_in, attn_sinks, w_attn_proj, ssm_lambda_re, ssm_lambda_im, ssm_log_dt,
                  ssm_b_re, ssm_b_im, ssm_c_re, ssm_c_im, ssm_d, w_glu, b_glu, w_ssm_proj,
                  w_out, ln1_g, ln1_b, peer_w_q, peer_keys1, peer_keys2, peer_u, peer_v,
                  ln2_g, ln2_b, ple_w_gate, ple_w_proj):
    b, l, _ = x.shape
    proj = x @ w_in
    q, k, v, u, ga, gb = jnp.split(proj, SPLIT_IDX, axis=-1)
    q = rope(q.reshape(b, l, N_HEADS, HEAD_DIM), pos)
    k = rope(k.reshape(b, l, N_KV_HEADS, HEAD_DIM), pos)
    v = v.reshape(b, l, N_KV_HEADS, HEAD_DIM)
    if win_k is None:
        a = attn_prompt(q, k, v, attn_sinks)
        new_k, new_v = k[:, -WINDOW:], v[:, -WINDOW:]
    else:
        a, new_k, new_v = attn_sample(q, k, v, win_k, win_v, attn_sinks)
    s, hr, hi = ssm_branch(u, h0_re, h0_im, ssm_lambda_re, ssm_lambda_im, ssm_log_dt,
                           ssm_b_re, ssm_b_im, ssm_c_re, ssm_c_im, ssm_d, w_glu, b_glu)
    mix = jax.nn.sigmoid(ga) * (a @ w_attn_proj) + jax.nn.sigmoid(gb) * (s @ w_ssm_proj)
    x = layer_norm(DN_ALPHA * x + mix @ w_out, ln1_g, ln1_b)
    x = layer_norm(DN_ALPHA * x + peer(x, peer_w_q, peer_keys1, peer_keys2, peer_u, peer_v), ln2_g, ln2_b)
    x = x + jax.nn.sigmoid(x @ ple_w_gate) * (p @ ple_w_proj)
    return x, new_k, new_v, hr, hi


def setup_inputs(seed: int = 0) -> dict:
    key = jax.random.key(seed)
    ks = jax.random.split(key, 40)
    f32 = jnp.float32

    def nrm(k, shape, scale):
        return jax.random.normal(k, shape, f32) * scale

    n_idx = jnp.arange(STATE_DIM, dtype=f32)
    ssm_shape = (DEPTH, N_GROUPS, STATE_DIM)
    return {
        'x_prompt': nrm(ks[0], (BATCH, SEQ, D_MODEL), 1.0),
        'x_sample': nrm(ks[1], (DEC_BATCH, DEC_SEQ, D_MODEL), 1.0),
        'p_prompt': nrm(ks[2], (DEPTH, BATCH, SEQ, PLE_DIM), 1.0),
        'p_sample': nrm(ks[3], (DEPTH, DEC_BATCH, DEC_SEQ, PLE_DIM), 1.0),
        'state_win_k': nrm(ks[4], (DEPTH, DEC_BATCH, WINDOW, N_KV_HEADS, HEAD_DIM), 1.0),
        'state_win_v': nrm(ks[5], (DEPTH, DEC_BATCH, WINDOW, N_KV_HEADS, HEAD_DIM), 1.0),
        'state_ssm_re': nrm(ks[6], (DEPTH, DEC_BATCH, N_GROUPS, STATE_DIM), 0.5),
        'state_ssm_im': nrm(ks[7], (DEPTH, DEC_BATCH, N_GROUPS, STATE_DIM), 0.5),
        'w_in': nrm(ks[8], (DEPTH, D_MODEL, IN_WIDTH), D_MODEL ** -0.5),
        'attn_sinks': nrm(ks[9], (DEPTH, N_HEADS), 1.0),
        'w_attn_proj': nrm(ks[10], (DEPTH, ATTN_WIDTH, D_MODEL), ATTN_WIDTH ** -0.5),
        'ssm_lambda_re': -0.5 + nrm(ks[11], ssm_shape, 0.01),
        'ssm_lambda_im': jnp.pi * n_idx + nrm(ks[12], ssm_shape, 0.01),
        'ssm_log_dt': jax.random.uniform(ks[13], (DEPTH, N_GROUPS), f32, math.log(DT_MIN), math.log(DT_MAX)),
        'ssm_b_re': nrm(ks[14], (DEPTH, N_GROUPS, STATE_DIM, GROUP_SIZE), (2 * GROUP_SIZE) ** -0.5),
        'ssm_b_im': nrm(ks[15], (DEPTH, N_GROUPS, STATE_DIM, GROUP_SIZE), (2 * GROUP_SIZE) ** -0.5),
        'ssm_c_re': nrm(ks[16], (DEPTH, N_GROUPS, GROUP_SIZE, STATE_DIM), (2 * STATE_DIM) ** -0.5),
        'ssm_c_im': nrm(ks[17], (DEPTH, N_GROUPS, GROUP_SIZE, STATE_DIM), (2 * STATE_DIM) ** -0.5),
        'ssm_d': nrm(ks[18], (DEPTH, SSM_WIDTH), 1.0),
        'w_glu': nrm(ks[19], (DEPTH, SSM_WIDTH, SSM_WIDTH), SSM_WIDTH ** -0.5),
        'b_glu': nrm(ks[20], (DEPTH, SSM_WIDTH), 0.02),
        'w_ssm_proj': nrm(ks[21], (DEPTH, SSM_WIDTH, D_MODEL), SSM_WIDTH ** -0.5),
        'w_out': nrm(ks[22], (DEPTH, D_MODEL, D_MODEL), DN_BETA * D_MODEL ** -0.5),
        'ln1_g': 1.0 + nrm(ks[23], (DEPTH, D_MODEL), 0.02),
        'ln1_b': nrm(ks[24], (DEPTH, D_MODEL), 0.02),
        'peer_w_q': nrm(ks[25], (DEPTH, D_MODEL, PEER_HEADS * 2 * KEY_DIM), D_MODEL ** -0.5),
        'peer_keys1': nrm(ks[26], (DEPTH, PEER_HEADS, N_KEYS, KEY_DIM), KEY_DIM ** -0.5),
        'peer_keys2': nrm(ks[27], (DEPTH, PEER_HEADS, N_KEYS, KEY_DIM), KEY_DIM ** -0.5),
        'peer_u': nrm(ks[28], (DEPTH, N_EXPERTS, D_MODEL), D_MODEL ** -0.5),
        'peer_v': nrm(ks[29], (DEPTH, N_EXPERTS, D_MODEL), DN_BETA * PEER_HEADS ** -0.5),
        'ln2_g': 1.0 + nrm(ks[30], (DEPTH, D_MODEL), 0.02),
        'ln2_b': nrm(ks[31], (DEPTH, D_MODEL), 0.02),
        'ple_w_gate': nrm(ks[32], (DEPTH, D_MODEL, D_MODEL), D_MODEL ** -0.5),
        'ple_w_proj': nrm(ks[33], (DEPTH, PLE_DIM, D_MODEL), PLE_DIM ** -0.5),
    }


def reference(x_prompt, x_sample, p_prompt, p_sample, state_win_k, state_win_v, state_ssm_re, state_ssm_im,
              w_in, attn_sinks, w_attn_proj, ssm_lambda_re, ssm_lambda_im, ssm_log_dt,
              ssm_b_re, ssm_b_im, ssm_c_re, ssm_c_im, ssm_d, w_glu, b_glu, w_ssm_proj,
              w_out, ln1_g, ln1_b, peer_w_q, peer_keys1, peer_keys2, peer_u, peer_v,
              ln2_g, ln2_b, ple_w_gate, ple_w_proj):
    pos_p = jnp.arange(x_prompt.shape[1], dtype=jnp.int32)
    pos_s = PAST_LEN + jnp.arange(x_sample.shape[1], dtype=jnp.int32)
    h_zero = jnp.zeros((x_prompt.shape[0], N_GROUPS, STATE_DIM), x_prompt.dtype)
    yp, ys = x_prompt, x_sample
    kp_l, vp_l, rp_l, ip_l, ks_l, vs_l, rs_l, is_l = [], [], [], [], [], [], [], []
    for i in range(DEPTH):
        w = (w_in[i], attn_sinks[i], w_attn_proj[i], ssm_lambda_re[i], ssm_lambda_im[i], ssm_log_dt[i],
             ssm_b_re[i], ssm_b_im[i], ssm_c_re[i], ssm_c_im[i], ssm_d[i], w_glu[i], b_glu[i], w_ssm_proj[i],
             w_out[i], ln1_g[i], ln1_b[i], peer_w_q[i], peer_keys1[i], peer_keys2[i], peer_u[i], peer_v[i],
             ln2_g[i], ln2_b[i], ple_w_gate[i], ple_w_proj[i])
        yp, kp, vp, rp, ip = decoder_layer(yp, p_prompt[i], pos_p, None, None, h_zero, h_zero, *w)
        ys, k_s, v_s, r_s, i_s = decoder_layer(ys, p_sample[i], pos_s, state_win_k[i], state_win_v[i],
                                               state_ssm_re[i], state_ssm_im[i], *w)
        kp_l.append(kp); vp_l.append(vp); rp_l.append(rp); ip_l.append(ip)
        ks_l.append(k_s); vs_l.append(v_s); rs_l.append(r_s); is_l.append(i_s)
    new_win_k_prompt = jnp.stack(kp_l)
    new_win_v_prompt = jnp.stack(vp_l)
    new_ssm_re_prompt = jnp.stack(rp_l)
    new_ssm_im_prompt = jnp.stack(ip_l)
    new_win_k_sample = jnp.stack(ks_l)
    new_win_v_sample = jnp.stack(vs_l)
    new_ssm_re_sample = jnp.stack(rs_l)
    new_ssm_im_sample = jnp.stack(is_l)
    return (yp, ys, new_win_k_prompt, new_win_v_prompt, new_ssm_re_prompt, new_ssm_im_prompt,
            new_win_k_sample, new_win_v_sample, new_ssm_re_sample, new_ssm_im_sample)
```

```python
import functools

import jax
import jax.numpy as jnp
from jax import lax
from jax.experimental import pallas as pl
from jax.experimental.pallas import tpu as pltpu

F32, BF16, I32 = jnp.float32, jnp.bfloat16, jnp.int32

D_MODEL = 1024
PAST_LEN = 16384
N_HEADS = 8
N_KV_HEADS = 2
HEAD_DIM = 64
WINDOW = 128
ROPE_THETA = 10000.0
ATTN_WIDTH = N_HEADS * HEAD_DIM
KV_WIDTH = N_KV_HEADS * HEAD_DIM
SSM_WIDTH = 512
GROUP_SIZE = 16
N_GROUPS = SSM_WIDTH // GROUP_SIZE
STATE_DIM = 64
SSM_STATES = N_GROUPS * STATE_DIM
PEER_HEADS = 8
N_KEYS = 128
KEY_DIM = 128
PEER_TOPK = 16
PEER_PICKS = PEER_HEADS * PEER_TOPK
PLE_DIM = 256
DN_ALPHA = 2.0 ** 0.25
LN_EPS = 1e-5

C_Q, C_K, C_V, C_U = 0, ATTN_WIDTH, ATTN_WIDTH + KV_WIDTH, ATTN_WIDTH + 2 * KV_WIDTH
C_GA = C_U + SSM_WIDTH
C_GB = C_GA + D_MODEL
IN_WIDTH = C_GB + D_MODEL

LANES = 128
SUBLANES = 8
HALF_D = D_MODEL // 2
ROW_TILES = HALF_D // LANES

TM_PROJ = 512
TM_MIX = 256
TM_FINAL = 256
T_SSM = 256
TK_TOPK = 128
TB_PEER = 128
MIB = 1024 * 1024


def _params(vmem_mib, n_axes=1):
    return pltpu.CompilerParams(dimension_semantics=("arbitrary",) * n_axes,
                                vmem_limit_bytes=vmem_mib * MIB)


def _full(shape):
    return pl.BlockSpec(shape, lambda *_: (0,) * len(shape))


def _dot(a, b):
    return jnp.dot(a, b, preferred_element_type=F32)


def _dot_nt(a, b):
    return lax.dot_general(a, b, (((1,), (1,)), ((), ())), preferred_element_type=F32)


def _gelu(x):
    return 0.5 * x * (1.0 + lax.erf(x * (2.0 ** -0.5)))


def _layer_norm(x, g, b):
    mu = jnp.mean(x, -1, keepdims=True)
    xc = x - mu
    var = jnp.mean(xc * xc, -1, keepdims=True)
    return xc * lax.rsqrt(var + LN_EPS) * g + b


def _proj_kernel(x_ref, w_ref, cos_ref, sin_ref, q_ref, k_ref, v_ref, u_ref, ga_ref, gb_ref):
    xb = x_ref[...].astype(BF16)
    cos = cos_ref[...]
    sin = sin_ref[...]
    lane = lax.broadcasted_iota(I32, cos.shape, 1)
    first = (lane & (HEAD_DIM - 1)) < HEAD_DIM // 2

    def rope(z):
        up = pltpu.roll(z, LANES - HEAD_DIM // 2, axis=1)
        dn = pltpu.roll(z, HEAD_DIM // 2, axis=1)
        return z * cos + jnp.where(first, up, dn) * sin

    for c in range(ATTN_WIDTH // LANES):
        lo = C_Q + c * LANES
        q_ref[:, c * LANES:(c + 1) * LANES] = rope(_dot(xb, w_ref[:, lo:lo + LANES]))
    k_ref[...] = rope(_dot(xb, w_ref[:, C_K:C_V]))
    v_ref[...] = _dot(xb, w_ref[:, C_V:C_U])
    u_ref[...] = _dot(xb, w_ref[:, C_U:C_GA])
    ga_ref[...] = jax.nn.sigmoid(_dot(xb, w_ref[:, C_GA:C_GB]))
    gb_ref[...] = jax.nn.sigmoid(_dot(xb, w_ref[:, C_GB:IN_WIDTH]))


def _proj(x_all, w_in_b, cos_t, sin_t, n_prompt_tiles, tiles_per_seq):
    t = x_all.shape[0]
    tm = TM_PROJ
    n = t // tm

    def pos_map(i):
        return (jnp.where(i < n_prompt_tiles, i % tiles_per_seq, tiles_per_seq), 0)

    row = lambda w: pl.BlockSpec((tm, w), lambda i: (i, 0))
    widths = (ATTN_WIDTH, KV_WIDTH, KV_WIDTH, SSM_WIDTH, D_MODEL, D_MODEL)
    return pl.pallas_call(
        _proj_kernel,
        grid=(n,),
        in_specs=[row(D_MODEL), _full(w_in_b.shape),
                  pl.BlockSpec((tm, LANES), pos_map), pl.BlockSpec((tm, LANES), pos_map)],
        out_specs=[row(w) for w in widths],
        out_shape=[jax.ShapeDtypeStruct((t, w), F32) for w in widths],
        compiler_params=_params(48),
        name="proj",
    )(x_all, w_in_b, cos_t, sin_t)


def _attn_core(q, kk, vv, mask, sink_ref):
    lane = lax.broadcasted_iota(I32, (1, LANES), 1)
    lo_half = lane < HEAD_DIM
    kb = kk.astype(BF16)
    kb_sw = pltpu.roll(kk, HEAD_DIM, axis=1).astype(BF16)
    vb = vv.astype(BF16)
    outs = []
    for c in range(ATTN_WIDTH // LANES):
        qc = q[:, c * LANES:(c + 1) * LANES]
        g = c // 2
        halves = []
        for a in range(2):
            h = 2 * c + a
            qm = jnp.where(lo_half if a == 0 else jnp.logical_not(lo_half), qc, 0.0).astype(BF16)
            s = _dot_nt(qm, kb if a == g else kb_sw) * (HEAD_DIM ** -0.5)
            s = jnp.where(mask, s, -jnp.inf)
            sink = sink_ref[h]
            m = jnp.maximum(jnp.max(s, -1, keepdims=True), sink)
            e = jnp.exp(s - m)
            denom = jnp.sum(e, -1, keepdims=True) + jnp.exp(sink - m)
            o = _dot((e / denom).astype(BF16), vb)
            halves.append(o if a == g else pltpu.roll(o, HEAD_DIM, axis=1))
        outs.append(jnp.where(lo_half, halves[0], halves[1]))
    return jnp.concatenate(outs, axis=1)


def _attn_prompt_kernel(sink_ref, q_ref, kp_ref, kc_ref, vp_ref, vc_ref, o_ref):
    j = pl.program_id(1)
    kk = jnp.concatenate([kp_ref[...], kc_ref[...]], axis=0)
    vv = jnp.concatenate([vp_ref[...], vc_ref[...]], axis=0)
    qi = lax.broadcasted_iota(I32, (WINDOW, 2 * WINDOW), 0)
    kj = lax.broadcasted_iota(I32, (WINDOW, 2 * WINDOW), 1)
    rel = qi + WINDOW - kj
    mask = (rel >= 0) & (rel < WINDOW) & ((kj >= WINDOW) | (j > 0))
    o_ref[...] = _attn_core(q_ref[...], kk, vv, mask, sink_ref)


def _attn_prompt(sinks, q, k, v, batch, nb):
    cur = lambda w: pl.BlockSpec((WINDOW, w), lambda b, j: (b * nb + j, 0))
    prev = lambda w: pl.BlockSpec((WINDOW, w), lambda b, j: (b * nb + jnp.maximum(j - 1, 0), 0))
    return pl.pallas_call(
        _attn_prompt_kernel,
        grid=(batch, nb),
        in_specs=[pl.BlockSpec(memory_space=pltpu.SMEM), cur(ATTN_WIDTH),
                  prev(KV_WIDTH), cur(KV_WIDTH), prev(KV_WIDTH), cur(KV_WIDTH)],
        out_specs=cur(ATTN_WIDTH),
        out_shape=jax.ShapeDtypeStruct((batch * nb * WINDOW, ATTN_WIDTH), F32),
        compiler_params=_params(32, 2),
        name="attn_prompt",
    )(sinks, q, k, k, v, v)


def _attn_sample_kernel(dec_seq, group, sink_ref, q_ref, kn_ref, vn_ref, kb_ref, vb_ref,
                        o_ref, nk_ref, nv_ref):
    qi = lax.broadcasted_iota(I32, (dec_seq, 2 * WINDOW), 0)
    kj = lax.broadcasted_iota(I32, (dec_seq, 2 * WINDOW), 1)
    rel = qi + WINDOW - kj
    mask = (rel >= 0) & (rel < WINDOW)
    pad = jnp.zeros((WINDOW - dec_seq, KV_WIDTH), F32)
    for b in range(group):
        rows = slice(b * dec_seq, (b + 1) * dec_seq)
        kk = jnp.concatenate([kb_ref[b], kn_ref[rows, :], pad], axis=0)
        vv = jnp.concatenate([vb_ref[b], vn_ref[rows, :], pad], axis=0)
        o_ref[rows, :] = _attn_core(q_ref[rows, :], kk, vv, mask, sink_ref)
        nk_ref[b] = kk[dec_seq:dec_seq + WINDOW]
        nv_ref[b] = vv[dec_seq:dec_seq + WINDOW]


def _attn_sample(sinks, q, k, v, k_buf, v_buf, row0, dec_batch, dec_seq):
    group = 8
    rows = group * dec_seq
    blk0 = row0 // rows
    tok = lambda w: pl.BlockSpec((rows, w), lambda i: (blk0 + i, 0))
    buf = pl.BlockSpec((group, WINDOW, KV_WIDTH), lambda i: (i, 0, 0))
    return pl.pallas_call(
        functools.partial(_attn_sample_kernel, dec_seq, group),
        grid=(dec_batch // group,),
        in_specs=[pl.BlockSpec(memory_space=pltpu.SMEM), tok(ATTN_WIDTH), tok(KV_WIDTH),
                  tok(KV_WIDTH), buf, buf],
        out_specs=[pl.BlockSpec((rows, ATTN_WIDTH), lambda i: (i, 0)), buf, buf],
        out_shape=[jax.ShapeDtypeStruct((dec_batch * dec_seq, ATTN_WIDTH), F32),
                   jax.ShapeDtypeStruct((dec_batch, WINDOW, KV_WIDTH), F32),
                   jax.ShapeDtypeStruct((dec_batch, WINDOW, KV_WIDTH), F32)],
        compiler_params=_params(32),
        name="attn_sample",
    )(sinks, q, k, v, k_buf, v_buf)


def _ssm_param_kernel(lr_ref, li_ref, ldt_ref, kre_ref, kim_ref, pwr_ref, pwi_ref):
    dt = jnp.exp(ldt_ref[...])
    lr = lr_ref[...]
    li = li_ref[...]
    mag = jnp.exp(lr * dt)
    a_re = mag * jnp.cos(li * dt)
    a_im = mag * jnp.sin(li * dt)
    nr, ni = a_re - 1.0, a_im
    den = lr * lr + li * li
    kre_ref[...] = (nr * lr + ni * li) / den
    kim_ref[...] = (ni * lr - nr * li) / den
    pr, pi = a_re, a_im
    for t in range(SUBLANES):
        pwr_ref[t] = pr
        pwi_ref[t] = pi
        pr, pi = pr * a_re - pi * a_im, pr * a_im + pi * a_re


def _ssm_params(lam_re, lam_im, log_dt):
    gp = (N_GROUPS, STATE_DIM)
    pw = (SUBLANES, N_GROUPS, STATE_DIM)
    return pl.pallas_call(
        _ssm_param_kernel,
        out_shape=[jax.ShapeDtypeStruct(gp, F32), jax.ShapeDtypeStruct(gp, F32),
                   jax.ShapeDtypeStruct(pw, F32), jax.ShapeDtypeStruct(pw, F32)],
        name="ssm_params",
    )(lam_re, lam_im, log_dt.reshape(N_GROUPS, 1))


def _ssm_kernel(sample, u_ref, bf_ref, cf_ref, kre_ref, kim_ref, pwr_ref, pwi_ref, d_ref,
                wglu_ref, bglu_ref, *rest):
    if sample:
        h0r_ref, h0i_ref, s_ref, hr_out, hi_out, xs, hs = rest
    else:
        s_ref, hr_out, hi_out, xs, hs, cr_s, ci_s = rest
    n = SSM_STATES
    u = u_ref[...]
    xs[...] = _dot(u.astype(BF16), bf_ref[...])
    row = lax.broadcasted_iota(I32, (SUBLANES, n), 0)
    kre, kim = kre_ref[...], kim_ref[...]
    pr, pi = pwr_ref[...], pwi_ref[...]
    steps = [(s, pwr_ref[s - 1:s, :], pwi_ref[s - 1:s, :]) for s in (1, 2, 4)]

    if not sample:
        @pl.when(pl.program_id(1) == 0)
        def _():
            cr_s[...] = jnp.zeros_like(cr_s)
            ci_s[...] = jnp.zeros_like(ci_s)

    def slab(j, carry):
        off = pl.multiple_of(j * SUBLANES, SUBLANES)
        br = xs[pl.ds(off, SUBLANES), 0:n]
        bi = xs[pl.ds(off, SUBLANES), n:2 * n]
        xr = kre * br - kim * bi
        xi = kre * bi + kim * br
        for s, ar, ai in steps:
            sr = jnp.where(row >= s, pltpu.roll(xr, s, axis=0), 0.0)
            si = jnp.where(row >= s, pltpu.roll(xi, s, axis=0), 0.0)
            xr, xi = xr + ar * sr - ai * si, xi + ar * si + ai * sr
        if sample:
            cr, ci = h0r_ref[pl.ds(j, 1), :], h0i_ref[pl.ds(j, 1), :]
        else:
            cr, ci = carry
        hr = xr + pr * cr - pi * ci
        hi = xi + pr * ci + pi * cr
        hs[pl.ds(off, SUBLANES), 0:n] = hr
        hs[pl.ds(off, SUBLANES), n:2 * n] = hi
        nr, ni = hr[SUBLANES - 1:SUBLANES, :], hi[SUBLANES - 1:SUBLANES, :]
        if sample:
            hr_out[pl.ds(j, 1), :] = nr
            hi_out[pl.ds(j, 1), :] = ni
            return carry
        return nr, ni

    n_slabs = u.shape[0] // SUBLANES
    if sample:
        lax.fori_loop(0, n_slabs, slab, 0)
    else:
        cr, ci = lax.fori_loop(0, n_slabs, slab, (cr_s[...], ci_s[...]))
        cr_s[...] = cr
        ci_s[...] = ci

        @pl.when(pl.program_id(1) == pl.num_programs(1) - 1)
        def _():
            hr_out[0] = cr
            hi_out[0] = ci

    y = _dot(hs[...].astype(BF16), cf_ref[...]) + d_ref[...] * u
    y = _gelu(y)
    z = _dot(y.astype(BF16), wglu_ref[...]) + bglu_ref[...]
    s_ref[...] = y * jax.nn.sigmoid(z)


def _ssm_weight_specs(n_axes):
    n = SSM_STATES
    shapes = [(SSM_WIDTH, 2 * n), (2 * n, SSM_WIDTH), (1, n), (1, n), (SUBLANES, n), (SUBLANES, n),
              (1, SSM_WIDTH), (SSM_WIDTH, SSM_WIDTH), (1, SSM_WIDTH)]
    return [_full(s) for s in shapes]


def _ssm_prompt(u, weights, batch, seq):
    tt = T_SSM
    nt = seq // tt
    n = SSM_STATES
    tok = pl.BlockSpec((tt, SSM_WIDTH), lambda b, t: (b * nt + t, 0))
    state = pl.BlockSpec((1, 1, n), lambda b, t: (b, 0, 0))
    return pl.pallas_call(
        functools.partial(_ssm_kernel, False),
        grid=(batch, nt),
        in_specs=[tok] + _ssm_weight_specs(2),
        out_specs=[tok, state, state],
        out_shape=[jax.ShapeDtypeStruct((batch * seq, SSM_WIDTH), F32),
                   jax.ShapeDtypeStruct((batch, 1, n), F32), jax.ShapeDtypeStruct((batch, 1, n), F32)],
        scratch_shapes=[pltpu.VMEM((tt, 2 * n), F32), pltpu.VMEM((tt, 2 * n), F32),
                        pltpu.VMEM((1, n), F32), pltpu.VMEM((1, n), F32)],
        compiler_params=_params(48, 2),
        name="ssm_prompt",
    )(u, *weights)


def _ssm_sample(u, weights, h0r, h0i, row0, dec_batch, dec_seq):
    assert dec_seq == SUBLANES
    tt = T_SSM
    nb = tt // dec_seq
    n = SSM_STATES
    blk0 = row0 // tt
    state = pl.BlockSpec((nb, n), lambda i: (i, 0))
    return pl.pallas_call(
        functools.partial(_ssm_kernel, True),
        grid=(dec_batch // nb,),
        in_specs=[pl.BlockSpec((tt, SSM_WIDTH), lambda i: (blk0 + i, 0))] + _ssm_weight_specs(1)
                 + [state, state],
        out_specs=[pl.BlockSpec((tt, SSM_WIDTH), lambda i: (i, 0)), state, state],
        out_shape=[jax.ShapeDtypeStruct((dec_batch * dec_seq, SSM_WIDTH), F32),
                   jax.ShapeDtypeStruct((dec_batch, n), F32), jax.ShapeDtypeStruct((dec_batch, n), F32)],
        scratch_shapes=[pltpu.VMEM((tt, 2 * n), F32), pltpu.VMEM((tt, 2 * n), F32)],
        compiler_params=_params(48),
        name="ssm_sample",
    )(u, *weights, h0r, h0i)


def _mix_kernel(a_ref, s_ref, ga_ref, gb_ref, x_ref, wap_ref, wsp_ref, wout_ref, g1_ref, b1_ref,
                wq_ref, k1_ref, k2_ref, x1_ref, sc_ref):
    mix = (ga_ref[...] * _dot(a_ref[...].astype(BF16), wap_ref[...])
           + gb_ref[...] * _dot(s_ref[...].astype(BF16), wsp_ref[...]))
    h = DN_ALPHA * x_ref[...] + _dot(mix.astype(BF16), wout_ref[...])
    x1 = _layer_norm(h, g1_ref[...], b1_ref[...])
    x1_ref[...] = x1
    x1b = x1.astype(BF16)
    for hd in range(PEER_HEADS):
        for half, keys_ref in enumerate((k1_ref, k2_ref)):
            lo = (hd * 2 + half) * KEY_DIM
            qh = _dot(x1b, wq_ref[:, lo:lo + KEY_DIM]).astype(BF16)
            r0 = (half * PEER_HEADS + hd) * N_KEYS
            sc_ref[r0:r0 + N_KEYS, :] = _dot_nt(keys_ref[hd], qh)


def _mix(a, s, ga, gb, x_all, wap, wsp, wout, g1, b1, wq, k1, k2):
    t = x_all.shape[0]
    tm = TM_MIX
    row = lambda w: pl.BlockSpec((tm, w), lambda i: (i, 0))
    n_rows = 2 * PEER_HEADS * N_KEYS
    return pl.pallas_call(
        _mix_kernel,
        grid=(t // tm,),
        in_specs=[row(ATTN_WIDTH), row(SSM_WIDTH), row(D_MODEL), row(D_MODEL), row(D_MODEL),
                  _full(wap.shape), _full(wsp.shape), _full(wout.shape), _full(g1.shape),
                  _full(b1.shape), _full(wq.shape), _full(k1.shape), _full(k2.shape)],
        out_specs=[row(D_MODEL), pl.BlockSpec((n_rows, tm), lambda i: (0, i))],
        out_shape=[jax.ShapeDtypeStruct((t, D_MODEL), F32), jax.ShapeDtypeStruct((n_rows, t), F32)],
        compiler_params=_params(48),
        name="mix",
    )(a, s, ga, gb, x_all, wap, wsp, wout, g1, b1, wq, k1, k2)


def _topk_rows(s, k):
    r = s.shape[0]
    iota = lax.broadcasted_iota(I32, s.shape, 0)
    vals, ids = [], []
    for _ in range(k):
        m = jnp.max(s, axis=0, keepdims=True)
        am = jnp.min(jnp.where(s == m, iota, r), axis=0, keepdims=True)
        vals.append(m)
        ids.append(am)
        s = jnp.where(iota == am, -jnp.inf, s)
    return jnp.concatenate(vals, axis=0), jnp.concatenate(ids, axis=0)


def _pick_rows(sel, table, k):
    out = jnp.zeros_like(sel)
    for i in range(k):
        out = out + jnp.where(sel == i, table[i:i + 1, :], 0)
    return out


def _topk_kernel(sc_ref, idx_ref, gate_ref):
    kk = PEER_TOPK
    ids, gates = [], []
    for hd in range(PEER_HEADS):
        r1 = hd * N_KEYS
        r2 = (PEER_HEADS + hd) * N_KEYS
        v1, i1 = _topk_rows(sc_ref[r1:r1 + N_KEYS, :], kk)
        v2, i2 = _topk_rows(sc_ref[r2:r2 + N_KEYS, :], kk)
        cand = jnp.concatenate([v1[i:i + 1, :] + v2 for i in range(kk)], axis=0)
        sc, ci = _topk_rows(cand, kk)
        e1 = _pick_rows(ci >> 4, i1, kk)
        e2 = _pick_rows(ci & (kk - 1), i2, kk)
        ids.append(e1 * N_KEYS + e2)
        e = jnp.exp(sc - jnp.max(sc, axis=0, keepdims=True))
        gates.append(e / jnp.sum(e, axis=0, keepdims=True))
    idx_ref[...] = jnp.concatenate(ids, axis=0).T
    gate_ref[...] = jnp.concatenate(gates, axis=0).T


def _topk(scores):
    n_rows, t = scores.shape
    tk = TK_TOPK
    out = pl.BlockSpec((tk, PEER_PICKS), lambda i: (i, 0))
    return pl.pallas_call(
        _topk_kernel,
        grid=(t // tk,),
        in_specs=[pl.BlockSpec((n_rows, tk), lambda i: (0, i))],
        out_specs=[out, out],
        out_shape=[jax.ShapeDtypeStruct((t, PEER_PICKS), I32), jax.ShapeDtypeStruct((t, PEER_PICKS), F32)],
        compiler_params=_params(32),
        name="topk",
    )(scores)


def _pack_table(tab):
    tb = tab.astype(BF16)
    hi = lax.bitcast_convert_type(tb[:, :HALF_D], jnp.uint16).astype(jnp.uint32)
    lo = lax.bitcast_convert_type(tb[:, HALF_D:], jnp.uint16).astype(jnp.uint32)
    words = lax.bitcast_convert_type((hi << 16) | lo, I32)
    return words.reshape(tab.shape[0] * ROW_TILES, LANES)


def _expert_row(tab, e):
    w = tab[pl.ds(pl.multiple_of(e * ROW_TILES, ROW_TILES), ROW_TILES), :]
    hi = lax.bitcast_convert_type(w & jnp.int32(-65536), F32)
    lo = lax.bitcast_convert_type(w << 16, F32)
    return hi, lo


def _load_tile_inputs(i, tab_hbm, tab, tab_sem, pairs):
    @pl.when(i == 0)
    def _():
        cp = pltpu.make_async_copy(tab_hbm, tab, tab_sem)
        cp.start()
        cp.wait()

    copies = [pltpu.make_async_copy(src.at[pl.ds(i * TB_PEER, TB_PEER)], dst, sem)
              for src, dst, sem in pairs]
    for cp in copies:
        cp.start()
    for cp in copies:
        cp.wait()


def _peer_u_kernel(x_ref, gate_ref, idx_hbm, tab_hbm, w_ref, tab, idx_s, qs, tab_sem, idx_sem):
    i = pl.program_id(0)
    _load_tile_inputs(i, tab_hbm, tab, tab_sem, [(idx_hbm, idx_s, idx_sem)])
    ones = jnp.ones((SUBLANES, LANES), F32)

    def token(t, _):
        x = x_ref[t]
        xa, xb = x[0:ROW_TILES], x[ROW_TILES:2 * ROW_TILES]
        for k in range(PEER_PICKS):
            hi, lo = _expert_row(tab, idx_s[t, k])
            qs[k:k + 1, :] = jnp.sum(hi * xa + lo * xb, axis=0, keepdims=True)
        h = lax.dot_general(ones, qs[...], (((1,), (1,)), ((), ())),
                            precision=lax.Precision.HIGHEST, preferred_element_type=F32)[0:1]
        w_ref[pl.ds(t, 1), :] = gate_ref[pl.ds(t, 1), :] * _gelu(h)
        return 0

    lax.fori_loop(0, TB_PEER, token, 0)


def _peer_u(x3, gate, idx, tab_u):
    t = x3.shape[0]
    tb = TB_PEER
    any_spec = pl.BlockSpec(memory_space=pl.ANY)
    row = pl.BlockSpec((tb, PEER_PICKS), lambda i: (i, 0))
    return pl.pallas_call(
        _peer_u_kernel,
        grid=(t // tb,),
        in_specs=[pl.BlockSpec((tb, SUBLANES, LANES), lambda i: (i, 0, 0)), row, any_spec, any_spec],
        out_specs=row,
        out_shape=jax.ShapeDtypeStruct((t, PEER_PICKS), F32),
        scratch_shapes=[pltpu.VMEM(tab_u.shape, I32), pltpu.SMEM((tb, PEER_PICKS), I32),
                        pltpu.VMEM((PEER_PICKS, LANES), F32),
                        pltpu.SemaphoreType.DMA, pltpu.SemaphoreType.DMA],
        compiler_params=_params(48),
        name="peer_u",
    )(x3, gate, idx, tab_u)


def _peer_v_kernel(idx_hbm, w_hbm, tab_hbm, y_ref, tab, idx_s, w_s, tab_sem, idx_sem, w_sem):
    i = pl.program_id(0)
    _load_tile_inputs(i, tab_hbm, tab, tab_sem, [(idx_hbm, idx_s, idx_sem), (w_hbm, w_s, w_sem)])
    n_acc = 4

    def token(t, _):
        acc = [jnp.zeros((ROW_TILES, LANES), F32) for _ in range(2 * n_acc)]
        for k in range(PEER_PICKS):
            hi, lo = _expert_row(tab, idx_s[t, k])
            w = w_s[t, k]
            a = 2 * (k % n_acc)
            acc[a] = acc[a] + w * hi
            acc[a + 1] = acc[a + 1] + w * lo
        hi = (acc[0] + acc[2]) + (acc[4] + acc[6])
        lo = (acc[1] + acc[3]) + (acc[5] + acc[7])
        y_ref[t] = jnp.concatenate([hi, lo], axis=0)
        return 0

    lax.fori_loop(0, TB_PEER, token, 0)


def _peer_v(idx, w, tab_v):
    t = idx.shape[0]
    tb = TB_PEER
    any_spec = pl.BlockSpec(memory_space=pl.ANY)
    return pl.pallas_call(
        _peer_v_kernel,
        grid=(t // tb,),
        in_specs=[any_spec, any_spec, any_spec],
        out_specs=pl.BlockSpec((tb, SUBLANES, LANES), lambda i: (i, 0, 0)),
        out_shape=jax.ShapeDtypeStruct((t, SUBLANES, LANES), F32),
        scratch_shapes=[pltpu.VMEM(tab_v.shape, I32), pltpu.SMEM((tb, PEER_PICKS), I32),
                        pltpu.SMEM((tb, PEER_PICKS), F32),
                        pltpu.SemaphoreType.DMA, pltpu.SemaphoreType.DMA, pltpu.SemaphoreType.DMA],
        compiler_params=_params(48),
        name="peer_v",
    )(idx, w, tab_v)


def _final_kernel(x1_ref, y_ref, p_ref, g2_ref, b2_ref, wg_ref, wp_ref, o_ref):
    x2 = _layer_norm(DN_ALPHA * x1_ref[...] + y_ref[...], g2_ref[...], b2_ref[...])
    gate = jax.nn.sigmoid(_dot(x2.astype(BF16), wg_ref[...]))
    o_ref[...] = x2 + gate * _dot(p_ref[...].astype(BF16), wp_ref[...])


def _final(x1, y, p_all, g2, b2, wg, wp):
    t = x1.shape[0]
    tm = TM_FINAL
    row = lambda w: pl.BlockSpec((tm, w), lambda i: (i, 0))
    return pl.pallas_call(
        _final_kernel,
        grid=(t // tm,),
        in_specs=[row(D_MODEL), row(D_MODEL), row(PLE_DIM), _full(g2.shape), _full(b2.shape),
                  _full(wg.shape), _full(wp.shape)],
        out_specs=row(D_MODEL),
        out_shape=jax.ShapeDtypeStruct((t, D_MODEL), F32),
        compiler_params=_params(32),
        name="final",
    )(x1, y, p_all, g2, b2, wg, wp)


def _rope_tables(seq, dec_seq, rows):
    half = HEAD_DIM // 2
    inv = ROPE_THETA ** (-jnp.arange(half, dtype=F32) / half)
    pos = jnp.concatenate([jnp.arange(seq, dtype=jnp.int32),
                           PAST_LEN + jnp.arange(rows, dtype=jnp.int32) % dec_seq])
    ang = pos.astype(F32)[:, None] * inv[None, :]
    cos = jnp.tile(jnp.cos(ang), (1, LANES // half))
    sin = jnp.tile(jnp.sin(ang), (1, LANES // half))
    first = (jnp.arange(LANES) % HEAD_DIM) < half
    return cos, jnp.where(first[None, :], -sin, sin)


def _block_diag(w):
    g, a, b = w.shape
    eye = jnp.eye(g, dtype=w.dtype)
    return jnp.einsum('gab,gh->gahb', w, eye).reshape(g * a, g * b)


def _layer(x_all, p_all, win_k, win_v, h0_re, h0_im, batch, seq, dec_batch, dec_seq,
           w_in, attn_sinks, w_attn_proj, lam_re, lam_im, log_dt, b_re, b_im, c_re, c_im, ssm_d,
           w_glu, b_glu, w_ssm_proj, w_out, ln1_g, ln1_b, peer_w_q, keys1, keys2, peer_u, peer_v,
           ln2_g, ln2_b, ple_w_gate, ple_w_proj):
    n_prompt = batch * seq
    row = lambda v: v.reshape(1, -1)

    cos_t, sin_t = _rope_tables(seq, dec_seq, TM_PROJ)
    q, k, v, u, ga, gb = _proj(x_all, w_in.astype(BF16), cos_t, sin_t,
                               n_prompt // TM_PROJ, seq // TM_PROJ)

    a_p = _attn_prompt(attn_sinks, q, k, v, batch, seq // WINDOW)
    a_s, new_k_s, new_v_s = _attn_sample(attn_sinks, q, k, v,
                                         win_k.reshape(dec_batch, WINDOW, KV_WIDTH),
                                         win_v.reshape(dec_batch, WINDOW, KV_WIDTH),
                                         n_prompt, dec_batch, dec_seq)

    k_re, k_im, pw_re, pw_im = _ssm_params(lam_re, lam_im, log_dt)
    n = SSM_STATES
    b_full = jnp.concatenate([_block_diag(jnp.swapaxes(b_re, 1, 2)),
                              _block_diag(jnp.swapaxes(b_im, 1, 2))], axis=1).astype(BF16)
    c_full = jnp.concatenate([_block_diag(jnp.swapaxes(c_re, 1, 2)),
                              _block_diag(jnp.swapaxes(-c_im, 1, 2))], axis=0).astype(BF16)
    ssm_w = (b_full, c_full, k_re.reshape(1, n), k_im.reshape(1, n),
             pw_re.reshape(SUBLANES, n), pw_im.reshape(SUBLANES, n), row(ssm_d),
             w_glu.astype(BF16), row(b_glu))
    s_p, hr_p, hi_p = _ssm_prompt(u, ssm_w, batch, seq)
    s_s, hr_s, hi_s = _ssm_sample(u, ssm_w, h0_re.reshape(dec_batch, n), h0_im.reshape(dec_batch, n),
                                  n_prompt, dec_batch, dec_seq)

    a_all = jnp.concatenate([a_p, a_s], axis=0)
    s_all = jnp.concatenate([s_p, s_s], axis=0)
    x1, scores = _mix(a_all, s_all, ga, gb, x_all, w_attn_proj.astype(BF16), w_ssm_proj.astype(BF16),
                      w_out.astype(BF16), row(ln1_g), row(ln1_b), peer_w_q.astype(BF16),
                      keys1.astype(BF16), keys2.astype(BF16))
    idx, gate = _topk(scores)
    w = _peer_u(x1.reshape(-1, SUBLANES, LANES), gate, idx, _pack_table(peer_u))
    y = _peer_v(idx, w, _pack_table(peer_v)).reshape(-1, D_MODEL)
    out = _final(x1, y, p_all, row(ln2_g), row(ln2_b), ple_w_gate.astype(BF16), ple_w_proj.astype(BF16))

    kv5 = lambda z: z.reshape(-1, WINDOW, N_KV_HEADS, HEAD_DIM)
    st = lambda z: z.reshape(-1, N_GROUPS, STATE_DIM)
    last = lambda z: z[:n_prompt].reshape(batch, seq, KV_WIDTH)[:, seq - WINDOW:]
    return (out, kv5(last(k)), kv5(last(v)), st(hr_p), st(hi_p),
            kv5(new_k_s), kv5(new_v_s), st(hr_s), st(hi_s))


def kernel(x_prompt, x_sample, p_prompt, p_sample, state_win_k, state_win_v, state_ssm_re, state_ssm_im,
           w_in, attn_sinks, w_attn_proj, ssm_lambda_re, ssm_lambda_im, ssm_log_dt,
           ssm_b_re, ssm_b_im, ssm_c_re, ssm_c_im, ssm_d, w_glu, b_glu, w_ssm_proj,
           w_out, ln1_g, ln1_b, peer_w_q, peer_keys1, peer_keys2, peer_u, peer_v,
           ln2_g, ln2_b, ple_w_gate, ple_w_proj):
    batch, seq, _ = x_prompt.shape
    dec_batch, dec_seq, _ = x_sample.shape
    depth = w_in.shape[0]
    n_prompt = batch * seq
    weights = (w_in, attn_sinks, w_attn_proj, ssm_lambda_re, ssm_lambda_im, ssm_log_dt,
               ssm_b_re, ssm_b_im, ssm_c_re, ssm_c_im, ssm_d, w_glu, b_glu, w_ssm_proj,
               w_out, ln1_g, ln1_b, peer_w_q, peer_keys1, peer_keys2, peer_u, peer_v,
               ln2_g, ln2_b, ple_w_gate, ple_w_proj)
    x_all = jnp.concatenate([x_prompt.reshape(n_prompt, D_MODEL), x_sample.reshape(-1, D_MODEL)], axis=0)
    per_layer = []
    for i in range(depth):
        p_all = jnp.concatenate([p_prompt[i].reshape(n_prompt, PLE_DIM),
                                 p_sample[i].reshape(-1, PLE_DIM)], axis=0)
        x_all, *states = _layer(x_all, p_all, state_win_k[i], state_win_v[i], state_ssm_re[i],
                                state_ssm_im[i], batch, seq, dec_batch, dec_seq, *(w[i] for w in weights))
        per_layer.append(states)
    stacked = [jnp.stack(z) for z in zip(*per_layer)]
    y_prompt = x_all[:n_prompt].reshape(batch, seq, D_MODEL)
    y_sample = x_all[n_prompt:].reshape(dec_batch, dec_seq, D_MODEL)
    return (y_prompt, y_sample, *stacked)
```

```python
import functools

import jax
import jax.numpy as jnp
from jax import lax
from jax.experimental import pallas as pl
from jax.experimental.pallas import tpu as pltpu

F32, BF16, I32 = jnp.float32, jnp.bfloat16, jnp.int32

D_MODEL = 1024
PAST_LEN = 16384
N_HEADS = 8
N_KV_HEADS = 2
HEAD_DIM = 64
WINDOW = 128
ROPE_THETA = 10000.0
ATTN_WIDTH = N_HEADS * HEAD_DIM
KV_WIDTH = N_KV_HEADS * HEAD_DIM
SSM_WIDTH = 512
GROUP_SIZE = 16
N_GROUPS = SSM_WIDTH // GROUP_SIZE
STATE_DIM = 64
SSM_STATES = N_GROUPS * STATE_DIM
PEER_HEADS = 8
N_KEYS = 128
KEY_DIM = 128
PEER_TOPK = 16
PEER_PICKS = PEER_HEADS * PEER_TOPK
PLE_DIM = 256
DN_ALPHA = 2.0 ** 0.25
LN_EPS = 1e-5

C_Q, C_K, C_V, C_U = 0, ATTN_WIDTH, ATTN_WIDTH + KV_WIDTH, ATTN_WIDTH + 2 * KV_WIDTH
C_GA = C_U + SSM_WIDTH
C_GB = C_GA + D_MODEL
IN_WIDTH = C_GB + D_MODEL

LANES = 128
SUBLANES = 8
ROW_TILES = D_MODEL // (2 * LANES)
G_ROWS = PEER_PICKS * ROW_TILES
PICK_COLS = PEER_PICKS * SUBLANES

TM_PROJ = 512
TM_MIX = 256
TM_FINAL = 256
T_SSM = 256
TK_TOPK = 128
TB_PEER = 256
PEER_UNROLL = 8
MIB = 1024 * 1024


def _params(vmem_mib, n_axes=1):
    return pltpu.CompilerParams(dimension_semantics=("arbitrary",) * n_axes,
                                vmem_limit_bytes=vmem_mib * MIB)


def _full(shape):
    return pl.BlockSpec(shape, lambda *_: (0,) * len(shape))


def _dot(a, b):
    return jnp.dot(a, b, preferred_element_type=F32)


def _dot_nt(a, b):
    return lax.dot_general(a, b, (((1,), (1,)), ((), ())), preferred_element_type=F32)


def _gelu(x):
    return 0.5 * x * (1.0 + lax.erf(x * (2.0 ** -0.5)))


def _layer_norm(x, g, b):
    mu = jnp.mean(x, -1, keepdims=True)
    xc = x - mu
    var = jnp.mean(xc * xc, -1, keepdims=True)
    return xc * lax.rsqrt(var + LN_EPS) * g + b


def _proj_kernel(x_ref, w_ref, cos_ref, sin_ref, q_ref, k_ref, v_ref, u_ref, ga_ref, gb_ref):
    xb = x_ref[...].astype(BF16)
    cos = cos_ref[...]
    sin = sin_ref[...]
    lane = lax.broadcasted_iota(I32, cos.shape, 1)
    first = (lane & (HEAD_DIM - 1)) < HEAD_DIM // 2

    def rope(z):
        up = pltpu.roll(z, LANES - HEAD_DIM // 2, axis=1)
        dn = pltpu.roll(z, HEAD_DIM // 2, axis=1)
        return z * cos + jnp.where(first, up, dn) * sin

    for c in range(ATTN_WIDTH // LANES):
        lo = C_Q + c * LANES
        q_ref[:, c * LANES:(c + 1) * LANES] = rope(_dot(xb, w_ref[:, lo:lo + LANES]))
    k_ref[...] = rope(_dot(xb, w_ref[:, C_K:C_V]))
    v_ref[...] = _dot(xb, w_ref[:, C_V:C_U])
    u_ref[...] = _dot(xb, w_ref[:, C_U:C_GA])
    ga_ref[...] = jax.nn.sigmoid(_dot(xb, w_ref[:, C_GA:C_GB]))
    gb_ref[...] = jax.nn.sigmoid(_dot(xb, w_ref[:, C_GB:IN_WIDTH]))


def _proj(x_all, w_in_b, cos_t, sin_t, n_prompt_tiles, tiles_per_seq):
    t = x_all.shape[0]
    tm = TM_PROJ
    n = t // tm

    def pos_map(i):
        return (jnp.where(i < n_prompt_tiles, i % tiles_per_seq, tiles_per_seq), 0)

    row = lambda w: pl.BlockSpec((tm, w), lambda i: (i, 0))
    widths = (ATTN_WIDTH, KV_WIDTH, KV_WIDTH, SSM_WIDTH, D_MODEL, D_MODEL)
    return pl.pallas_call(
        _proj_kernel,
        grid=(n,),
        in_specs=[row(D_MODEL), _full(w_in_b.shape),
                  pl.BlockSpec((tm, LANES), pos_map), pl.BlockSpec((tm, LANES), pos_map)],
        out_specs=[row(w) for w in widths],
        out_shape=[jax.ShapeDtypeStruct((t, w), F32) for w in widths],
        compiler_params=_params(48),
        name="proj",
    )(x_all, w_in_b, cos_t, sin_t)


def _attn_core(q, kk, vv, mask, sink_ref):
    lane = lax.broadcasted_iota(I32, (1, LANES), 1)
    lo_half = lane < HEAD_DIM
    kb = kk.astype(BF16)
    kb_sw = pltpu.roll(kk, HEAD_DIM, axis=1).astype(BF16)
    vb = vv.astype(BF16)
    outs = []
    for c in range(ATTN_WIDTH // LANES):
        qc = q[:, c * LANES:(c + 1) * LANES]
        g = c // 2
        halves = []
        for a in range(2):
            h = 2 * c + a
            qm = jnp.where(lo_half if a == 0 else jnp.logical_not(lo_half), qc, 0.0).astype(BF16)
            s = _dot_nt(qm, kb if a == g else kb_sw) * (HEAD_DIM ** -0.5)
            s = jnp.where(mask, s, -jnp.inf)
            sink = sink_ref[h]
            m = jnp.maximum(jnp.max(s, -1, keepdims=True), sink)
            e = jnp.exp(s - m)
            denom = jnp.sum(e, -1, keepdims=True) + jnp.exp(sink - m)
            o = _dot((e / denom).astype(BF16), vb)
            halves.append(o if a == g else pltpu.roll(o, HEAD_DIM, axis=1))
        outs.append(jnp.where(lo_half, halves[0], halves[1]))
    return jnp.concatenate(outs, axis=1)


def _attn_prompt_kernel(sink_ref, q_ref, kp_ref, kc_ref, vp_ref, vc_ref, o_ref):
    j = pl.program_id(1)
    kk = jnp.concatenate([kp_ref[...], kc_ref[...]], axis=0)
    vv = jnp.concatenate([vp_ref[...], vc_ref[...]], axis=0)
    qi = lax.broadcasted_iota(I32, (WINDOW, 2 * WINDOW), 0)
    kj = lax.broadcasted_iota(I32, (WINDOW, 2 * WINDOW), 1)
    rel = qi + WINDOW - kj
    mask = (rel >= 0) & (rel < WINDOW) & ((kj >= WINDOW) | (j > 0))
    o_ref[...] = _attn_core(q_ref[...], kk, vv, mask, sink_ref)


def _attn_prompt(sinks, q, k, v, batch, nb):
    cur = lambda w: pl.BlockSpec((WINDOW, w), lambda b, j: (b * nb + j, 0))
    prev = lambda w: pl.BlockSpec((WINDOW, w), lambda b, j: (b * nb + jnp.maximum(j - 1, 0), 0))
    return pl.pallas_call(
        _attn_prompt_kernel,
        grid=(batch, nb),
        in_specs=[pl.BlockSpec(memory_space=pltpu.SMEM), cur(ATTN_WIDTH),
                  prev(KV_WIDTH), cur(KV_WIDTH), prev(KV_WIDTH), cur(KV_WIDTH)],
        out_specs=cur(ATTN_WIDTH),
        out_shape=jax.ShapeDtypeStruct((batch * nb * WINDOW, ATTN_WIDTH), F32),
        compiler_params=_params(32, 2),
        name="attn_prompt",
    )(sinks, q, k, k, v, v)


def _attn_sample_kernel(dec_seq, group, sink_ref, q_ref, kn_ref, vn_ref, kb_ref, vb_ref,
                        o_ref, nk_ref, nv_ref):
    qi = lax.broadcasted_iota(I32, (dec_seq, 2 * WINDOW), 0)
    kj = lax.broadcasted_iota(I32, (dec_seq, 2 * WINDOW), 1)
    rel = qi + WINDOW - kj
    mask = (rel >= 0) & (rel < WINDOW)
    pad = jnp.zeros((WINDOW - dec_seq, KV_WIDTH), F32)
    for b in range(group):
        rows = slice(b * dec_seq, (b + 1) * dec_seq)
        kk = jnp.concatenate([kb_ref[b], kn_ref[rows, :], pad], axis=0)
        vv = jnp.concatenate([vb_ref[b], vn_ref[rows, :], pad], axis=0)
        o_ref[rows, :] = _attn_core(q_ref[rows, :], kk, vv, mask, sink_ref)
        nk_ref[b] = kk[dec_seq:dec_seq + WINDOW]
        nv_ref[b] = vv[dec_seq:dec_seq + WINDOW]


def _attn_sample(sinks, q, k, v, k_buf, v_buf, row0, dec_batch, dec_seq):
    group = 8
    rows = group * dec_seq
    blk0 = row0 // rows
    tok = lambda w: pl.BlockSpec((rows, w), lambda i: (blk0 + i, 0))
    buf = pl.BlockSpec((group, WINDOW, KV_WIDTH), lambda i: (i, 0, 0))
    return pl.pallas_call(
        functools.partial(_attn_sample_kernel, dec_seq, group),
        grid=(dec_batch // group,),
        in_specs=[pl.BlockSpec(memory_space=pltpu.SMEM), tok(ATTN_WIDTH), tok(KV_WIDTH),
                  tok(KV_WIDTH), buf, buf],
        out_specs=[pl.BlockSpec((rows, ATTN_WIDTH), lambda i: (i, 0)), buf, buf],
        out_shape=[jax.ShapeDtypeStruct((dec_batch * dec_seq, ATTN_WIDTH), F32),
                   jax.ShapeDtypeStruct((dec_batch, WINDOW, KV_WIDTH), F32),
                   jax.ShapeDtypeStruct((dec_batch, WINDOW, KV_WIDTH), F32)],
        compiler_params=_params(32),
        name="attn_sample",
    )(sinks, q, k, v, k_buf, v_buf)


def _ssm_param_kernel(lr_ref, li_ref, ldt_ref, kre_ref, kim_ref, pwr_ref, pwi_ref):
    dt = jnp.exp(ldt_ref[...])
    lr = lr_ref[...]
    li = li_ref[...]
    mag = jnp.exp(lr * dt)
    a_re = mag * jnp.cos(li * dt)
    a_im = mag * jnp.sin(li * dt)
    nr, ni = a_re - 1.0, a_im
    den = lr * lr + li * li
    kre_ref[...] = (nr * lr + ni * li) / den
    kim_ref[...] = (ni * lr - nr * li) / den
    pr, pi = a_re, a_im
    for t in range(SUBLANES):
        pwr_ref[t] = pr
        pwi_ref[t] = pi
        pr, pi = pr * a_re - pi * a_im, pr * a_im + pi * a_re


def _ssm_params(lam_re, lam_im, log_dt):
    gp = (N_GROUPS, STATE_DIM)
    pw = (SUBLANES, N_GROUPS, STATE_DIM)
    return pl.pallas_call(
        _ssm_param_kernel,
        out_shape=[jax.ShapeDtypeStruct(gp, F32), jax.ShapeDtypeStruct(gp, F32),
                   jax.ShapeDtypeStruct(pw, F32), jax.ShapeDtypeStruct(pw, F32)],
        name="ssm_params",
    )(lam_re, lam_im, log_dt.reshape(N_GROUPS, 1))


def _ssm_kernel(sample, u_ref, bf_ref, cf_ref, kre_ref, kim_ref, pwr_ref, pwi_ref, d_ref,
                wglu_ref, bglu_ref, *rest):
    if sample:
        h0r_ref, h0i_ref, s_ref, hr_out, hi_out, xs, hs = rest
    else:
        s_ref, hr_out, hi_out, xs, hs, cr_s, ci_s = rest
    n = SSM_STATES
    u = u_ref[...]
    xs[...] = _dot(u.astype(BF16), bf_ref[...])
    row = lax.broadcasted_iota(I32, (SUBLANES, n), 0)
    kre, kim = kre_ref[...], kim_ref[...]
    pr, pi = pwr_ref[...], pwi_ref[...]
    steps = [(s, pwr_ref[s - 1:s, :], pwi_ref[s - 1:s, :]) for s in (1, 2, 4)]

    if not sample:
        @pl.when(pl.program_id(1) == 0)
        def _():
            cr_s[...] = jnp.zeros_like(cr_s)
            ci_s[...] = jnp.zeros_like(ci_s)

    def slab(j, carry):
        off = pl.multiple_of(j * SUBLANES, SUBLANES)
        br = xs[pl.ds(off, SUBLANES), 0:n]
        bi = xs[pl.ds(off, SUBLANES), n:2 * n]
        xr = kre * br - kim * bi
        xi = kre * bi + kim * br
        for s, ar, ai in steps:
            sr = jnp.where(row >= s, pltpu.roll(xr, s, axis=0), 0.0)
            si = jnp.where(row >= s, pltpu.roll(xi, s, axis=0), 0.0)
            xr, xi = xr + ar * sr - ai * si, xi + ar * si + ai * sr
        if sample:
            cr, ci = h0r_ref[pl.ds(j, 1), :], h0i_ref[pl.ds(j, 1), :]
        else:
            cr, ci = carry
        hr = xr + pr * cr - pi * ci
        hi = xi + pr * ci + pi * cr
        hs[pl.ds(off, SUBLANES), 0:n] = hr
        hs[pl.ds(off, SUBLANES), n:2 * n] = hi
        nr, ni = hr[SUBLANES - 1:SUBLANES, :], hi[SUBLANES - 1:SUBLANES, :]
        if sample:
            hr_out[pl.ds(j, 1), :] = nr
            hi_out[pl.ds(j, 1), :] = ni
            return carry
        return nr, ni

    n_slabs = u.shape[0] // SUBLANES
    if sample:
        lax.fori_loop(0, n_slabs, slab, 0)
    else:
        cr, ci = lax.fori_loop(0, n_slabs, slab, (cr_s[...], ci_s[...]))
        cr_s[...] = cr
        ci_s[...] = ci

        @pl.when(pl.program_id(1) == pl.num_programs(1) - 1)
        def _():
            hr_out[0] = cr
            hi_out[0] = ci

    y = _dot(hs[...].astype(BF16), cf_ref[...]) + d_ref[...] * u
    y = _gelu(y)
    z = _dot(y.astype(BF16), wglu_ref[...]) + bglu_ref[...]
    s_ref[...] = y * jax.nn.sigmoid(z)


def _ssm_weight_specs(n_axes):
    n = SSM_STATES
    shapes = [(SSM_WIDTH, 2 * n), (2 * n, SSM_WIDTH), (1, n), (1, n), (SUBLANES, n), (SUBLANES, n),
              (1, SSM_WIDTH), (SSM_WIDTH, SSM_WIDTH), (1, SSM_WIDTH)]
    return [_full(s) for s in shapes]


def _ssm_prompt(u, weights, batch, seq):
    tt = T_SSM
    nt = seq // tt
    n = SSM_STATES
    tok = pl.BlockSpec((tt, SSM_WIDTH), lambda b, t: (b * nt + t, 0))
    state = pl.BlockSpec((1, 1, n), lambda b, t: (b, 0, 0))
    return pl.pallas_call(
        functools.partial(_ssm_kernel, False),
        grid=(batch, nt),
        in_specs=[tok] + _ssm_weight_specs(2),
        out_specs=[tok, state, state],
        out_shape=[jax.ShapeDtypeStruct((batch * seq, SSM_WIDTH), F32),
                   jax.ShapeDtypeStruct((batch, 1, n), F32), jax.ShapeDtypeStruct((batch, 1, n), F32)],
        scratch_shapes=[pltpu.VMEM((tt, 2 * n), F32), pltpu.VMEM((tt, 2 * n), F32),
                        pltpu.VMEM((1, n), F32), pltpu.VMEM((1, n), F32)],
        compiler_params=_params(48, 2),
        name="ssm_prompt",
    )(u, *weights)


def _ssm_sample(u, weights, h0r, h0i, row0, dec_batch, dec_seq):
    assert dec_seq == SUBLANES
    tt = T_SSM
    nb = tt // dec_seq
    n = SSM_STATES
    blk0 = row0 // tt
    state = pl.BlockSpec((nb, n), lambda i: (i, 0))
    return pl.pallas_call(
        functools.partial(_ssm_kernel, True),
        grid=(dec_batch // nb,),
        in_specs=[pl.BlockSpec((tt, SSM_WIDTH), lambda i: (blk0 + i, 0))] + _ssm_weight_specs(1)
                 + [state, state],
        out_specs=[pl.BlockSpec((tt, SSM_WIDTH), lambda i: (i, 0)), state, state],
        out_shape=[jax.ShapeDtypeStruct((dec_batch * dec_seq, SSM_WIDTH), F32),
                   jax.ShapeDtypeStruct((dec_batch, n), F32), jax.ShapeDtypeStruct((dec_batch, n), F32)],
        scratch_shapes=[pltpu.VMEM((tt, 2 * n), F32), pltpu.VMEM((tt, 2 * n), F32)],
        compiler_params=_params(48),
        name="ssm_sample",
    )(u, *weights, h0r, h0i)


def _mix_kernel(a_ref, s_ref, ga_ref, gb_ref, x_ref, wap_ref, wsp_ref, wout_ref, g1_ref, b1_ref,
                wq_ref, k1_ref, k2_ref, x1_ref, sc_ref):
    mix = (ga_ref[...] * _dot(a_ref[...].astype(BF16), wap_ref[...])
           + gb_ref[...] * _dot(s_ref[...].astype(BF16), wsp_ref[...]))
    h = DN_ALPHA * x_ref[...] + _dot(mix.astype(BF16), wout_ref[...])
    x1 = _layer_norm(h, g1_ref[...], b1_ref[...])
    x1_ref[...] = x1
    x1b = x1.astype(BF16)
    for hd in range(PEER_HEADS):
        for half, keys_ref in enumerate((k1_ref, k2_ref)):
            lo = (hd * 2 + half) * KEY_DIM
            qh = _dot(x1b, wq_ref[:, lo:lo + KEY_DIM]).astype(BF16)
            r0 = (half * PEER_HEADS + hd) * N_KEYS
            sc_ref[r0:r0 + N_KEYS, :] = _dot_nt(keys_ref[hd], qh)


def _mix(a, s, ga, gb, x_all, wap, wsp, wout, g1, b1, wq, k1, k2):
    t = x_all.shape[0]
    tm = TM_MIX
    row = lambda w: pl.BlockSpec((tm, w), lambda i: (i, 0))
    n_rows = 2 * PEER_HEADS * N_KEYS
    return pl.pallas_call(
        _mix_kernel,
        grid=(t // tm,),
        in_specs=[row(ATTN_WIDTH), row(SSM_WIDTH), row(D_MODEL), row(D_MODEL), row(D_MODEL),
                  _full(wap.shape), _full(wsp.shape), _full(wout.shape), _full(g1.shape),
                  _full(b1.shape), _full(wq.shape), _full(k1.shape), _full(k2.shape)],
        out_specs=[row(D_MODEL), pl.BlockSpec((n_rows, tm), lambda i: (0, i))],
        out_shape=[jax.ShapeDtypeStruct((t, D_MODEL), F32), jax.ShapeDtypeStruct((n_rows, t), F32)],
        compiler_params=_params(48),
        name="mix",
    )(a, s, ga, gb, x_all, wap, wsp, wout, g1, b1, wq, k1, k2)


def _topk_rows(s, k, code=None):
    if code is None:
        code = lax.broadcasted_iota(I32, s.shape, 0)
    big = jnp.iinfo(jnp.int32).max
    vals, ids = [], []
    for _ in range(k):
        m = jnp.max(s, axis=0, keepdims=True)
        am = jnp.min(jnp.where(s == m, code, big), axis=0, keepdims=True)
        vals.append(m)
        ids.append(am)
        s = jnp.where(code == am, -jnp.inf, s)
    return jnp.concatenate(vals, axis=0), jnp.concatenate(ids, axis=0)


def _pick_rows(sel, table, k):
    out = jnp.zeros_like(sel)
    for i in range(k):
        out = out + jnp.where(sel == i, table[i:i + 1, :], 0)
    return out


def _topk_kernel(sc_ref, idx_ref, gate_ref):
    kk = PEER_TOPK
    half = kk // 2
    tk = sc_ref.shape[1]
    row16 = lax.broadcasted_iota(I32, (kk, tk), 0)
    row8 = lax.broadcasted_iota(I32, (half, tk), 0)
    code = jnp.concatenate([row16] + [kk * i + row8 for i in range(1, half)] + [kk * (half + row8)], axis=0)
    ids, gates = [], []
    for hd in range(PEER_HEADS):
        r1 = hd * N_KEYS
        r2 = (PEER_HEADS + hd) * N_KEYS
        v1, i1 = _topk_rows(sc_ref[r1:r1 + N_KEYS, :], kk)
        v2, i2 = _topk_rows(sc_ref[r2:r2 + N_KEYS, :], kk)
        cand = jnp.concatenate([v1[0:1] + v2] + [v1[i:i + 1] + v2[0:half] for i in range(1, half)]
                               + [v1[half:kk] + v2[0:1]], axis=0)
        sc, ci = _topk_rows(cand, kk, code)
        e1 = _pick_rows(ci >> 4, i1, kk)
        e2 = _pick_rows(ci & (kk - 1), i2, kk)
        ids.append((e1 * N_KEYS + e2) * ROW_TILES)
        e = jnp.exp(sc - jnp.max(sc, axis=0, keepdims=True))
        gates.append(e / jnp.sum(e, axis=0, keepdims=True))
    idx_ref[...] = jnp.concatenate(ids, axis=0).T
    gate_ref[...] = jnp.concatenate(gates, axis=0).T


def _topk(scores):
    n_rows, t = scores.shape
    tk = TK_TOPK
    out = pl.BlockSpec((tk, PEER_PICKS), lambda i: (i, 0))
    return pl.pallas_call(
        _topk_kernel,
        grid=(t // tk,),
        in_specs=[pl.BlockSpec((n_rows, tk), lambda i: (0, i))],
        out_specs=[out, out],
        out_shape=[jax.ShapeDtypeStruct((t, PEER_PICKS), I32), jax.ShapeDtypeStruct((t, PEER_PICKS), F32)],
        compiler_params=_params(32),
        name="topk",
    )(scores)


def _pack_table(tab):
    e = tab.shape[0]
    tb = tab.astype(BF16).reshape(e, ROW_TILES, 2, LANES)
    return lax.bitcast_convert_type(jnp.swapaxes(tb, 2, 3), I32).reshape(e * ROW_TILES, LANES)


def _peer_constants():
    n = jnp.arange(PICK_COLS)
    mask = (n[None, :] % SUBLANES == jnp.arange(SUBLANES)[:, None]).astype(F32)
    sel = (n[:, None] // SUBLANES == jnp.arange(PEER_PICKS)[None, :]).astype(BF16)
    return mask, sel


def _split(x):
    hi = x.astype(BF16)
    return hi, (x - hi.astype(F32)).astype(BF16)


def _stack_split(x):
    hi = x.astype(BF16).astype(F32)
    return jnp.concatenate([hi, x - hi], axis=0).astype(BF16)


def _fold(y):
    return y[0:SUBLANES] + y[SUBLANES:2 * SUBLANES]


def _idx_copy(idx_hbm, idx_s, sems, tile, slot):
    return pltpu.make_async_copy(idx_hbm.at[pl.ds(tile * TB_PEER, TB_PEER)], idx_s.at[slot], sems.at[slot])


def _fetch_tile(tab_hbm, tab, tab_sem, idx_hbm, idx_s, idx_sems):
    i = pl.program_id(0)
    slot = i % 2

    @pl.when(i == 0)
    def _():
        cp = pltpu.make_async_copy(tab_hbm, tab, tab_sem)
        cp.start()
        _idx_copy(idx_hbm, idx_s, idx_sems, 0, 0).start()
        cp.wait()

    @pl.when(i + 1 < pl.num_programs(0))
    def _():
        _idx_copy(idx_hbm, idx_s, idx_sems, i + 1, 1 - slot).start()

    _idx_copy(idx_hbm, idx_s, idx_sems, i, slot).wait()
    return slot


def _gather_rows(tab, idx_s, slot, t, g_ref):
    for k in range(PEER_PICKS):
        e4 = pl.multiple_of(idx_s[slot, t, k], ROW_TILES)
        g_ref[k * ROW_TILES:(k + 1) * ROW_TILES, :] = tab[pl.ds(e4, ROW_TILES), :]


def _token_groups(tab, idx_s, slot, bufs, compute):
    n = len(bufs)
    _gather_rows(tab, idx_s, slot, 0, bufs[0])

    def group(p, _):
        for j in range(n):
            t = p * n + j
            compute(t, bufs[j])
            _gather_rows(tab, idx_s, slot, jnp.minimum(t + 1, TB_PEER - 1), bufs[(j + 1) % n])
        return 0

    lax.fori_loop(0, TB_PEER // n, group, 0)


def _peer_u_kernel(x_ref, gate_ref, mask_ref, sel_ref, idx_hbm, tab_hbm, w_ref,
                   tab, idx_s, r_s, tab_sem, idx_sems, *bufs):
    slot = _fetch_tile(tab_hbm, tab, tab_sem, idx_hbm, idx_s, idx_sems)
    mask = mask_ref[...]

    def compute(t, g_ref):
        x2 = _stack_split(x_ref[t])
        g = pltpu.bitcast(g_ref[...], BF16)
        out = _fold(_dot_nt(x2, g))
        r_s[pl.ds(t, 1), :] = jnp.sum(out * mask, axis=0, keepdims=True)

    _token_groups(tab, idx_s, slot, bufs, compute)
    rh, rl = _split(r_s[...])
    h = _dot(rh, sel_ref[...]) + _dot(rl, sel_ref[...])
    w_ref[...] = gate_ref[...] * _gelu(h)


def _peer_scratch(tab):
    return ([pltpu.VMEM(tab.shape, I32), pltpu.SMEM((2, TB_PEER, PEER_PICKS), I32),
             pltpu.VMEM((TB_PEER, PICK_COLS), F32),
             pltpu.SemaphoreType.DMA, pltpu.SemaphoreType.DMA((2,))]
            + [pltpu.VMEM((G_ROWS, LANES), I32)] * PEER_UNROLL)


def _peer_u(x3, gate, idx, tab_u, mask, sel):
    t = x3.shape[0]
    tb = TB_PEER
    any_spec = pl.BlockSpec(memory_space=pl.ANY)
    row = pl.BlockSpec((tb, PEER_PICKS), lambda i: (i, 0))
    return pl.pallas_call(
        _peer_u_kernel,
        grid=(t // tb,),
        in_specs=[pl.BlockSpec((tb, SUBLANES, LANES), lambda i: (i, 0, 0)), row, _full(mask.shape),
                  _full(sel.shape), any_spec, any_spec],
        out_specs=row,
        out_shape=jax.ShapeDtypeStruct((t, PEER_PICKS), F32),
        scratch_shapes=_peer_scratch(tab_u),
        compiler_params=_params(48),
        name="peer_u",
    )(x3, gate, mask, sel, idx, tab_u)


def _peer_v_kernel(w_ref, mask_ref, selt_ref, idx_hbm, tab_hbm, y_ref,
                   tab, idx_s, wx_s, tab_sem, idx_sems, *bufs):
    slot = _fetch_tile(tab_hbm, tab, tab_sem, idx_hbm, idx_s, idx_sems)
    mask = mask_ref[...]
    wh, wl = _split(w_ref[...])
    wx_s[...] = _dot(wh, selt_ref[...]) + _dot(wl, selt_ref[...])

    def compute(t, g_ref):
        a = jnp.broadcast_to(wx_s[pl.ds(t, 1), :], mask.shape) * mask
        g = pltpu.bitcast(g_ref[...], BF16)
        y_ref[t] = _fold(_dot(_stack_split(a), g))

    _token_groups(tab, idx_s, slot, bufs, compute)


def _peer_v(w, idx, tab_v, mask, selt):
    t = idx.shape[0]
    tb = TB_PEER
    any_spec = pl.BlockSpec(memory_space=pl.ANY)
    return pl.pallas_call(
        _peer_v_kernel,
        grid=(t // tb,),
        in_specs=[pl.BlockSpec((tb, PEER_PICKS), lambda i: (i, 0)), _full(mask.shape), _full(selt.shape),
                  any_spec, any_spec],
        out_specs=pl.BlockSpec((tb, SUBLANES, LANES), lambda i: (i, 0, 0)),
        out_shape=jax.ShapeDtypeStruct((t, SUBLANES, LANES), F32),
        scratch_shapes=_peer_scratch(tab_v),
        compiler_params=_params(48),
        name="peer_v",
    )(w, mask, selt, idx, tab_v)


def _final_kernel(x1_ref, y_ref, p_ref, g2_ref, b2_ref, wg_ref, wp_ref, o_ref):
    x2 = _layer_norm(DN_ALPHA * x1_ref[...] + y_ref[...], g2_ref[...], b2_ref[...])
    gate = jax.nn.sigmoid(_dot(x2.astype(BF16), wg_ref[...]))
    o_ref[...] = x2 + gate * _dot(p_ref[...].astype(BF16), wp_ref[...])


def _final(x1, y, p_all, g2, b2, wg, wp):
    t = x1.shape[0]
    tm = TM_FINAL
    row = lambda w: pl.BlockSpec((tm, w), lambda i: (i, 0))
    return pl.pallas_call(
        _final_kernel,
        grid=(t // tm,),
        in_specs=[row(D_MODEL), row(D_MODEL), row(PLE_DIM), _full(g2.shape), _full(b2.shape),
                  _full(wg.shape), _full(wp.shape)],
        out_specs=row(D_MODEL),
        out_shape=jax.ShapeDtypeStruct((t, D_MODEL), F32),
        compiler_params=_params(32),
        name="final",
    )(x1, y, p_all, g2, b2, wg, wp)


def _rope_tables(seq, dec_seq, rows):
    half = HEAD_DIM // 2
    inv = ROPE_THETA ** (-jnp.arange(half, dtype=F32) / half)
    pos = jnp.concatenate([jnp.arange(seq, dtype=jnp.int32),
                           PAST_LEN + jnp.arange(rows, dtype=jnp.int32) % dec_seq])
    ang = pos.astype(F32)[:, None] * inv[None, :]
    cos = jnp.tile(jnp.cos(ang), (1, LANES // half))
    sin = jnp.tile(jnp.sin(ang), (1, LANES // half))
    first = (jnp.arange(LANES) % HEAD_DIM) < half
    return cos, jnp.where(first[None, :], -sin, sin)


def _block_diag(w):
    g, a, b = w.shape
    eye = jnp.eye(g, dtype=w.dtype)
    return jnp.einsum('gab,gh->gahb', w, eye).reshape(g * a, g * b)


def _layer(x_all, p_all, win_k, win_v, h0_re, h0_im, batch, seq, dec_batch, dec_seq,
           w_in, attn_sinks, w_attn_proj, lam_re, lam_im, log_dt, b_re, b_im, c_re, c_im, ssm_d,
           w_glu, b_glu, w_ssm_proj, w_out, ln1_g, ln1_b, peer_w_q, keys1, keys2, peer_u, peer_v,
           ln2_g, ln2_b, ple_w_gate, ple_w_proj):
    n_prompt = batch * seq
    row = lambda v: v.reshape(1, -1)

    cos_t, sin_t = _rope_tables(seq, dec_seq, TM_PROJ)
    q, k, v, u, ga, gb = _proj(x_all, w_in.astype(BF16), cos_t, sin_t,
                               n_prompt // TM_PROJ, seq // TM_PROJ)

    a_p = _attn_prompt(attn_sinks, q, k, v, batch, seq // WINDOW)
    a_s, new_k_s, new_v_s = _attn_sample(attn_sinks, q, k, v,
                                         win_k.reshape(dec_batch, WINDOW, KV_WIDTH),
                                         win_v.reshape(dec_batch, WINDOW, KV_WIDTH),
                                         n_prompt, dec_batch, dec_seq)

    k_re, k_im, pw_re, pw_im = _ssm_params(lam_re, lam_im, log_dt)
    n = SSM_STATES
    b_full = jnp.concatenate([_block_diag(jnp.swapaxes(b_re, 1, 2)),
                              _block_diag(jnp.swapaxes(b_im, 1, 2))], axis=1).astype(BF16)
    c_full = jnp.concatenate([_block_diag(jnp.swapaxes(c_re, 1, 2)),
                              _block_diag(jnp.swapaxes(-c_im, 1, 2))], axis=0).astype(BF16)
    ssm_w = (b_full, c_full, k_re.reshape(1, n), k_im.reshape(1, n),
             pw_re.reshape(SUBLANES, n), pw_im.reshape(SUBLANES, n), row(ssm_d),
             w_glu.astype(BF16), row(b_glu))
    s_p, hr_p, hi_p = _ssm_prompt(u, ssm_w, batch, seq)
    s_s, hr_s, hi_s = _ssm_sample(u, ssm_w, h0_re.reshape(dec_batch, n), h0_im.reshape(dec_batch, n),
                                  n_prompt, dec_batch, dec_seq)

    a_all = jnp.concatenate([a_p, a_s], axis=0)
    s_all = jnp.concatenate([s_p, s_s], axis=0)
    x1, scores = _mix(a_all, s_all, ga, gb, x_all, w_attn_proj.astype(BF16), w_ssm_proj.astype(BF16),
                      w_out.astype(BF16), row(ln1_g), row(ln1_b), peer_w_q.astype(BF16),
                      keys1.astype(BF16), keys2.astype(BF16))
    idx, gate = _topk(scores)
    mask, sel = _peer_constants()
    w = _peer_u(x1.reshape(-1, SUBLANES, LANES), gate, idx, _pack_table(peer_u), mask, sel)
    y = _peer_v(w, idx, _pack_table(peer_v), mask, sel.T).reshape(-1, D_MODEL)
    out = _final(x1, y, p_all, row(ln2_g), row(ln2_b), ple_w_gate.astype(BF16), ple_w_proj.astype(BF16))

    kv5 = lambda z: z.reshape(-1, WINDOW, N_KV_HEADS, HEAD_DIM)
    st = lambda z: z.reshape(-1, N_GROUPS, STATE_DIM)
    last = lambda z: z[:n_prompt].reshape(batch, seq, KV_WIDTH)[:, seq - WINDOW:]
    return (out, kv5(last(k)), kv5(last(v)), st(hr_p), st(hi_p),
            kv5(new_k_s), kv5(new_v_s), st(hr_s), st(hi_s))


def kernel(x_prompt, x_sample, p_prompt, p_sample, state_win_k, state_win_v, state_ssm_re, state_ssm_im,
           w_in, attn_sinks, w_attn_proj, ssm_lambda_re, ssm_lambda_im, ssm_log_dt,
           ssm_b_re, ssm_b_im, ssm_c_re, ssm_c_im, ssm_d, w_glu, b_glu, w_ssm_proj,
           w_out, ln1_g, ln1_b, peer_w_q, peer_keys1, peer_keys2, peer_u, peer_v,
           ln2_g, ln2_b, ple_w_gate, ple_w_proj):
    batch, seq, _ = x_prompt.shape
    dec_batch, dec_seq, _ = x_sample.shape
    depth = w_in.shape[0]
    n_prompt = batch * seq
    weights = (w_in, attn_sinks, w_attn_proj, ssm_lambda_re, ssm_lambda_im, ssm_log_dt,
               ssm_b_re, ssm_b_im, ssm_c_re, ssm_c_im, ssm_d, w_glu, b_glu, w_ssm_proj,
               w_out, ln1_g, ln1_b, peer_w_q, peer_keys1, peer_keys2, peer_u, peer_v,
               ln2_g, ln2_b, ple_w_gate, ple_w_proj)
    x_all = jnp.concatenate([x_prompt.reshape(n_prompt, D_MODEL), x_sample.reshape(-1, D_MODEL)], axis=0)
    per_layer = []
    for i in range(depth):
        p_all = jnp.concatenate([p_prompt[i].reshape(n_prompt, PLE_DIM),
                                 p_sample[i].reshape(-1, PLE_DIM)], axis=0)
        x_all, *states = _layer(x_all, p_all, state_win_k[i], state_win_v[i], state_ssm_re[i],
                                state_ssm_im[i], batch, seq, dec_batch, dec_seq, *(w[i] for w in weights))
        per_layer.append(states)
    stacked = [jnp.stack(z) for z in zip(*per_layer)]
    y_prompt = x_all[:n_prompt].reshape(batch, seq, D_MODEL)
    y_sample = x_all[n_prompt:].reshape(dec_batch, dec_seq, D_MODEL)
    return (y_prompt, y_sample, *stacked)
```

```python
import functools

import jax
import jax.numpy as jnp
from jax import lax
from jax.experimental import pallas as pl
from jax.experimental.pallas import tpu as pltpu

F32, BF16, I32 = jnp.float32, jnp.bfloat16, jnp.int32

D_MODEL = 1024
PAST_LEN = 16384
N_HEADS = 8
N_KV_HEADS = 2
HEAD_DIM = 64
WINDOW = 128
ROPE_THETA = 10000.0
ATTN_WIDTH = N_HEADS * HEAD_DIM
KV_WIDTH = N_KV_HEADS * HEAD_DIM
SSM_WIDTH = 512
GROUP_SIZE = 16
N_GROUPS = SSM_WIDTH // GROUP_SIZE
STATE_DIM = 64
SSM_STATES = N_GROUPS * STATE_DIM
PEER_HEADS = 8
N_KEYS = 128
KEY_DIM = 128
PEER_TOPK = 16
PEER_PICKS = PEER_HEADS * PEER_TOPK
PLE_DIM = 256
DN_ALPHA = 2.0 ** 0.25
LN_EPS = 1e-5

C_Q, C_K, C_V, C_U = 0, ATTN_WIDTH, ATTN_WIDTH + KV_WIDTH, ATTN_WIDTH + 2 * KV_WIDTH
C_GA = C_U + SSM_WIDTH
C_GB = C_GA + D_MODEL
IN_WIDTH = C_GB + D_MODEL

LANES = 128
SUBLANES = 8
ROW_TILES = D_MODEL // (2 * LANES)
G_ROWS = PEER_PICKS * ROW_TILES
PICK_COLS = PEER_PICKS * SUBLANES

TM_PROJ = 512
TM_MIX = 256
TM_FINAL = 256
T_SSM = 256
TK_TOPK = 128
TB_PEER = 256
PEER_UNROLL = 16
IDX_ROWS = PEER_PICKS * (TB_PEER // LANES)
MIB = 1024 * 1024


def _params(vmem_mib, n_axes=1):
    return pltpu.CompilerParams(dimension_semantics=("arbitrary",) * n_axes,
                                vmem_limit_bytes=vmem_mib * MIB)


def _full(shape):
    return pl.BlockSpec(shape, lambda *_: (0,) * len(shape))


def _two_part(tm, width, n_first):
    return [pl.BlockSpec((tm, width), lambda i: (jnp.minimum(i, n_first - 1), 0)),
            pl.BlockSpec((tm, width), lambda i: (jnp.maximum(i - n_first, 0), 0))]


def _part(n_first, first_ref, second_ref):
    return jnp.where(pl.program_id(0) < n_first, first_ref[...], second_ref[...])


def _dot(a, b):
    return jnp.dot(a, b, preferred_element_type=F32)


def _dot_nt(a, b):
    return lax.dot_general(a, b, (((1,), (1,)), ((), ())), preferred_element_type=F32)


def _gelu(x):
    return 0.5 * x * (1.0 + lax.erf(x * (2.0 ** -0.5)))


def _layer_norm(x, g, b):
    mu = jnp.mean(x, -1, keepdims=True)
    xc = x - mu
    var = jnp.mean(xc * xc, -1, keepdims=True)
    return xc * lax.rsqrt(var + LN_EPS) * g + b


def _proj_kernel(n_first, xp_ref, xs_ref, w_ref, cos_ref, sin_ref, q_ref, k_ref, v_ref, u_ref, ga_ref, gb_ref):
    xb = _part(n_first, xp_ref, xs_ref).astype(BF16)
    cos = cos_ref[...]
    sin = sin_ref[...]
    lane = lax.broadcasted_iota(I32, cos.shape, 1)
    first = (lane & (HEAD_DIM - 1)) < HEAD_DIM // 2

    def rope(z):
        up = pltpu.roll(z, LANES - HEAD_DIM // 2, axis=1)
        dn = pltpu.roll(z, HEAD_DIM // 2, axis=1)
        return z * cos + jnp.where(first, up, dn) * sin

    for c in range(ATTN_WIDTH // LANES):
        lo = C_Q + c * LANES
        q_ref[:, c * LANES:(c + 1) * LANES] = rope(_dot(xb, w_ref[:, lo:lo + LANES]))
    k_ref[...] = rope(_dot(xb, w_ref[:, C_K:C_V]))
    v_ref[...] = _dot(xb, w_ref[:, C_V:C_U])
    u_ref[...] = _dot(xb, w_ref[:, C_U:C_GA])
    ga_ref[...] = jax.nn.sigmoid(_dot(xb, w_ref[:, C_GA:C_GB]))
    gb_ref[...] = jax.nn.sigmoid(_dot(xb, w_ref[:, C_GB:IN_WIDTH]))


def _proj(x_p, x_s, w_in_b, cos_t, sin_t, tiles_per_seq):
    t = x_p.shape[0] + x_s.shape[0]
    tm = TM_PROJ
    n = t // tm
    n_prompt_tiles = x_p.shape[0] // tm

    def pos_map(i):
        return (jnp.where(i < n_prompt_tiles, i % tiles_per_seq, tiles_per_seq), 0)

    row = lambda w: pl.BlockSpec((tm, w), lambda i: (i, 0))
    widths = (ATTN_WIDTH, KV_WIDTH, KV_WIDTH, SSM_WIDTH, D_MODEL, D_MODEL)
    return pl.pallas_call(
        functools.partial(_proj_kernel, n_prompt_tiles),
        grid=(n,),
        in_specs=_two_part(tm, D_MODEL, n_prompt_tiles) + [_full(w_in_b.shape),
                  pl.BlockSpec((tm, LANES), pos_map), pl.BlockSpec((tm, LANES), pos_map)],
        out_specs=[row(w) for w in widths],
        out_shape=[jax.ShapeDtypeStruct((t, w), F32) for w in widths],
        compiler_params=_params(48),
        name="proj",
    )(x_p, x_s, w_in_b, cos_t, sin_t)


def _attn_core(q, kk, vv, mask, sink_ref):
    lane = lax.broadcasted_iota(I32, (1, LANES), 1)
    lo_half = lane < HEAD_DIM
    kb = kk.astype(BF16)
    kb_sw = pltpu.roll(kk, HEAD_DIM, axis=1).astype(BF16)
    vb = vv.astype(BF16)
    outs = []
    for c in range(ATTN_WIDTH // LANES):
        qc = q[:, c * LANES:(c + 1) * LANES]
        g = c // 2
        halves = []
        for a in range(2):
            h = 2 * c + a
            qm = jnp.where(lo_half if a == 0 else jnp.logical_not(lo_half), qc, 0.0).astype(BF16)
            s = _dot_nt(qm, kb if a == g else kb_sw) * (HEAD_DIM ** -0.5)
            s = jnp.where(mask, s, -jnp.inf)
            sink = sink_ref[h]
            m = jnp.maximum(jnp.max(s, -1, keepdims=True), sink)
            e = jnp.exp(s - m)
            denom = jnp.sum(e, -1, keepdims=True) + jnp.exp(sink - m)
            o = _dot((e / denom).astype(BF16), vb)
            halves.append(o if a == g else pltpu.roll(o, HEAD_DIM, axis=1))
        outs.append(jnp.where(lo_half, halves[0], halves[1]))
    return jnp.concatenate(outs, axis=1)


def _attn_prompt_kernel(sink_ref, q_ref, kp_ref, kc_ref, vp_ref, vc_ref, o_ref):
    j = pl.program_id(1)
    kk = jnp.concatenate([kp_ref[...], kc_ref[...]], axis=0)
    vv = jnp.concatenate([vp_ref[...], vc_ref[...]], axis=0)
    qi = lax.broadcasted_iota(I32, (WINDOW, 2 * WINDOW), 0)
    kj = lax.broadcasted_iota(I32, (WINDOW, 2 * WINDOW), 1)
    rel = qi + WINDOW - kj
    mask = (rel >= 0) & (rel < WINDOW) & ((kj >= WINDOW) | (j > 0))
    o_ref[...] = _attn_core(q_ref[...], kk, vv, mask, sink_ref)


def _attn_prompt(sinks, q, k, v, batch, nb):
    cur = lambda w: pl.BlockSpec((WINDOW, w), lambda b, j: (b * nb + j, 0))
    prev = lambda w: pl.BlockSpec((WINDOW, w), lambda b, j: (b * nb + jnp.maximum(j - 1, 0), 0))
    return pl.pallas_call(
        _attn_prompt_kernel,
        grid=(batch, nb),
        in_specs=[pl.BlockSpec(memory_space=pltpu.SMEM), cur(ATTN_WIDTH),
                  prev(KV_WIDTH), cur(KV_WIDTH), prev(KV_WIDTH), cur(KV_WIDTH)],
        out_specs=cur(ATTN_WIDTH),
        out_shape=jax.ShapeDtypeStruct((batch * nb * WINDOW, ATTN_WIDTH), F32),
        compiler_params=_params(32, 2),
        name="attn_prompt",
    )(sinks, q, k, k, v, v)


def _attn_sample_kernel(dec_seq, group, sink_ref, q_ref, kn_ref, vn_ref, kb_ref, vb_ref,
                        o_ref, nk_ref, nv_ref):
    qi = lax.broadcasted_iota(I32, (dec_seq, 2 * WINDOW), 0)
    kj = lax.broadcasted_iota(I32, (dec_seq, 2 * WINDOW), 1)
    rel = qi + WINDOW - kj
    mask = (rel >= 0) & (rel < WINDOW)
    pad = jnp.zeros((WINDOW - dec_seq, KV_WIDTH), F32)
    for b in range(group):
        rows = slice(b * dec_seq, (b + 1) * dec_seq)
        kk = jnp.concatenate([kb_ref[b], kn_ref[rows, :], pad], axis=0)
        vv = jnp.concatenate([vb_ref[b], vn_ref[rows, :], pad], axis=0)
        o_ref[rows, :] = _attn_core(q_ref[rows, :], kk, vv, mask, sink_ref)
        nk_ref[b] = kk[dec_seq:dec_seq + WINDOW]
        nv_ref[b] = vv[dec_seq:dec_seq + WINDOW]


def _attn_sample(sinks, q, k, v, k_buf, v_buf, row0, dec_batch, dec_seq):
    group = 8
    rows = group * dec_seq
    blk0 = row0 // rows
    tok = lambda w: pl.BlockSpec((rows, w), lambda i: (blk0 + i, 0))
    buf = pl.BlockSpec((group, WINDOW, KV_WIDTH), lambda i: (i, 0, 0))
    return pl.pallas_call(
        functools.partial(_attn_sample_kernel, dec_seq, group),
        grid=(dec_batch // group,),
        in_specs=[pl.BlockSpec(memory_space=pltpu.SMEM), tok(ATTN_WIDTH), tok(KV_WIDTH),
                  tok(KV_WIDTH), buf, buf],
        out_specs=[pl.BlockSpec((rows, ATTN_WIDTH), lambda i: (i, 0)), buf, buf],
        out_shape=[jax.ShapeDtypeStruct((dec_batch * dec_seq, ATTN_WIDTH), F32),
                   jax.ShapeDtypeStruct((dec_batch, WINDOW, KV_WIDTH), F32),
                   jax.ShapeDtypeStruct((dec_batch, WINDOW, KV_WIDTH), F32)],
        compiler_params=_params(32),
        name="attn_sample",
    )(sinks, q, k, v, k_buf, v_buf)


def _ssm_param_kernel(lr_ref, li_ref, ldt_ref, kre_ref, kim_ref, pwr_ref, pwi_ref):
    dt = jnp.exp(ldt_ref[...])
    lr = lr_ref[...]
    li = li_ref[...]
    mag = jnp.exp(lr * dt)
    a_re = mag * jnp.cos(li * dt)
    a_im = mag * jnp.sin(li * dt)
    nr, ni = a_re - 1.0, a_im
    den = lr * lr + li * li
    kre_ref[...] = (nr * lr + ni * li) / den
    kim_ref[...] = (ni * lr - nr * li) / den
    pr, pi = a_re, a_im
    for t in range(SUBLANES):
        pwr_ref[t] = pr
        pwi_ref[t] = pi
        pr, pi = pr * a_re - pi * a_im, pr * a_im + pi * a_re


def _ssm_params(lam_re, lam_im, log_dt):
    gp = (N_GROUPS, STATE_DIM)
    pw = (SUBLANES, N_GROUPS, STATE_DIM)
    return pl.pallas_call(
        _ssm_param_kernel,
        out_shape=[jax.ShapeDtypeStruct(gp, F32), jax.ShapeDtypeStruct(gp, F32),
                   jax.ShapeDtypeStruct(pw, F32), jax.ShapeDtypeStruct(pw, F32)],
        name="ssm_params",
    )(lam_re, lam_im, log_dt.reshape(N_GROUPS, 1))


def _ssm_kernel(sample, u_ref, bf_ref, cf_ref, kre_ref, kim_ref, pwr_ref, pwi_ref, d_ref,
                wglu_ref, bglu_ref, *rest):
    if sample:
        h0r_ref, h0i_ref, s_ref, hr_out, hi_out, xs, hs = rest
    else:
        s_ref, hr_out, hi_out, xs, hs, cr_s, ci_s = rest
    n = SSM_STATES
    u = u_ref[...]
    xs[...] = _dot(u.astype(BF16), bf_ref[...])
    row = lax.broadcasted_iota(I32, (SUBLANES, n), 0)
    kre, kim = kre_ref[...], kim_ref[...]
    pr, pi = pwr_ref[...], pwi_ref[...]
    steps = [(s, pwr_ref[s - 1:s, :], pwi_ref[s - 1:s, :]) for s in (1, 2, 4)]

    if not sample:
        @pl.when(pl.program_id(1) == 0)
        def _():
            cr_s[...] = jnp.zeros_like(cr_s)
            ci_s[...] = jnp.zeros_like(ci_s)

    def slab(j, carry):
        off = pl.multiple_of(j * SUBLANES, SUBLANES)
        br = xs[pl.ds(off, SUBLANES), 0:n]
        bi = xs[pl.ds(off, SUBLANES), n:2 * n]
        xr = kre * br - kim * bi
        xi = kre * bi + kim * br
        for s, ar, ai in steps:
            sr = jnp.where(row >= s, pltpu.roll(xr, s, axis=0), 0.0)
            si = jnp.where(row >= s, pltpu.roll(xi, s, axis=0), 0.0)
            xr, xi = xr + ar * sr - ai * si, xi + ar * si + ai * sr
        if sample:
            cr, ci = h0r_ref[pl.ds(j, 1), :], h0i_ref[pl.ds(j, 1), :]
        else:
            cr, ci = carry
        hr = xr + pr * cr - pi * ci
        hi = xi + pr * ci + pi * cr
        hs[pl.ds(off, SUBLANES), 0:n] = hr
        hs[pl.ds(off, SUBLANES), n:2 * n] = hi
        nr, ni = hr[SUBLANES - 1:SUBLANES, :], hi[SUBLANES - 1:SUBLANES, :]
        if sample:
            hr_out[pl.ds(j, 1), :] = nr
            hi_out[pl.ds(j, 1), :] = ni
            return carry
        return nr, ni

    n_slabs = u.shape[0] // SUBLANES
    if sample:
        lax.fori_loop(0, n_slabs, slab, 0)
    else:
        cr, ci = lax.fori_loop(0, n_slabs, slab, (cr_s[...], ci_s[...]))
        cr_s[...] = cr
        ci_s[...] = ci

        @pl.when(pl.program_id(1) == pl.num_programs(1) - 1)
        def _():
            hr_out[0] = cr
            hi_out[0] = ci

    y = _dot(hs[...].astype(BF16), cf_ref[...]) + d_ref[...] * u
    y = _gelu(y)
    z = _dot(y.astype(BF16), wglu_ref[...]) + bglu_ref[...]
    s_ref[...] = y * jax.nn.sigmoid(z)


def _ssm_weight_specs(n_axes):
    n = SSM_STATES
    shapes = [(SSM_WIDTH, 2 * n), (2 * n, SSM_WIDTH), (1, n), (1, n), (SUBLANES, n), (SUBLANES, n),
              (1, SSM_WIDTH), (SSM_WIDTH, SSM_WIDTH), (1, SSM_WIDTH)]
    return [_full(s) for s in shapes]


def _ssm_prompt(u, weights, batch, seq):
    tt = T_SSM
    nt = seq // tt
    n = SSM_STATES
    tok = pl.BlockSpec((tt, SSM_WIDTH), lambda b, t: (b * nt + t, 0))
    state = pl.BlockSpec((1, 1, n), lambda b, t: (b, 0, 0))
    return pl.pallas_call(
        functools.partial(_ssm_kernel, False),
        grid=(batch, nt),
        in_specs=[tok] + _ssm_weight_specs(2),
        out_specs=[tok, state, state],
        out_shape=[jax.ShapeDtypeStruct((batch * seq, SSM_WIDTH), F32),
                   jax.ShapeDtypeStruct((batch, 1, n), F32), jax.ShapeDtypeStruct((batch, 1, n), F32)],
        scratch_shapes=[pltpu.VMEM((tt, 2 * n), F32), pltpu.VMEM((tt, 2 * n), F32),
                        pltpu.VMEM((1, n), F32), pltpu.VMEM((1, n), F32)],
        compiler_params=_params(48, 2),
        name="ssm_prompt",
    )(u, *weights)


def _ssm_sample(u, weights, h0r, h0i, row0, dec_batch, dec_seq):
    assert dec_seq == SUBLANES
    tt = T_SSM
    nb = tt // dec_seq
    n = SSM_STATES
    blk0 = row0 // tt
    state = pl.BlockSpec((nb, n), lambda i: (i, 0))
    return pl.pallas_call(
        functools.partial(_ssm_kernel, True),
        grid=(dec_batch // nb,),
        in_specs=[pl.BlockSpec((tt, SSM_WIDTH), lambda i: (blk0 + i, 0))] + _ssm_weight_specs(1)
                 + [state, state],
        out_specs=[pl.BlockSpec((tt, SSM_WIDTH), lambda i: (i, 0)), state, state],
        out_shape=[jax.ShapeDtypeStruct((dec_batch * dec_seq, SSM_WIDTH), F32),
                   jax.ShapeDtypeStruct((dec_batch, n), F32), jax.ShapeDtypeStruct((dec_batch, n), F32)],
        scratch_shapes=[pltpu.VMEM((tt, 2 * n), F32), pltpu.VMEM((tt, 2 * n), F32)],
        compiler_params=_params(48),
        name="ssm_sample",
    )(u, *weights, h0r, h0i)


def _mix_kernel(n_first, ap_ref, as_ref, sp_ref, ss_ref, xp_ref, xs_ref, ga_ref, gb_ref,
                wap_ref, wsp_ref, wout_ref, g1_ref, b1_ref, wq_ref, k1_ref, k2_ref, x1_ref, sc_ref):
    a = _part(n_first, ap_ref, as_ref)
    s = _part(n_first, sp_ref, ss_ref)
    mix = (ga_ref[...] * _dot(a.astype(BF16), wap_ref[...])
           + gb_ref[...] * _dot(s.astype(BF16), wsp_ref[...]))
    h = DN_ALPHA * _part(n_first, xp_ref, xs_ref) + _dot(mix.astype(BF16), wout_ref[...])
    x1 = _layer_norm(h, g1_ref[...], b1_ref[...])
    x1_ref[...] = x1
    x1b = x1.astype(BF16)
    for hd in range(PEER_HEADS):
        for half, keys_ref in enumerate((k1_ref, k2_ref)):
            lo = (hd * 2 + half) * KEY_DIM
            qh = _dot(x1b, wq_ref[:, lo:lo + KEY_DIM]).astype(BF16)
            r0 = (half * PEER_HEADS + hd) * N_KEYS
            sc_ref[r0:r0 + N_KEYS, :] = _dot_nt(keys_ref[hd], qh)


def _mix(a_p, a_s, s_p, s_s, x_p, x_s, ga, gb, wap, wsp, wout, g1, b1, wq, k1, k2):
    t = x_p.shape[0] + x_s.shape[0]
    tm = TM_MIX
    n_first = x_p.shape[0] // tm
    row = lambda w: pl.BlockSpec((tm, w), lambda i: (i, 0))
    n_rows = 2 * PEER_HEADS * N_KEYS
    return pl.pallas_call(
        functools.partial(_mix_kernel, n_first),
        grid=(t // tm,),
        in_specs=_two_part(tm, ATTN_WIDTH, n_first) + _two_part(tm, SSM_WIDTH, n_first)
                 + _two_part(tm, D_MODEL, n_first) + [row(D_MODEL), row(D_MODEL),
                  _full(wap.shape), _full(wsp.shape), _full(wout.shape), _full(g1.shape),
                  _full(b1.shape), _full(wq.shape), _full(k1.shape), _full(k2.shape)],
        out_specs=[row(D_MODEL), pl.BlockSpec((n_rows, tm), lambda i: (0, i))],
        out_shape=[jax.ShapeDtypeStruct((t, D_MODEL), F32), jax.ShapeDtypeStruct((n_rows, t), F32)],
        compiler_params=_params(48),
        name="mix",
    )(a_p, a_s, s_p, s_s, x_p, x_s, ga, gb, wap, wsp, wout, g1, b1, wq, k1, k2)


def _topk_rows(s, k, code=None):
    if code is None:
        code = lax.broadcasted_iota(I32, s.shape, 0)
    big = jnp.iinfo(jnp.int32).max
    vals, ids = [], []
    for _ in range(k):
        m = jnp.max(s, axis=0, keepdims=True)
        am = jnp.min(jnp.where(s == m, code, big), axis=0, keepdims=True)
        vals.append(m)
        ids.append(am)
        s = jnp.where(code == am, -jnp.inf, s)
    return jnp.concatenate(vals, axis=0), jnp.concatenate(ids, axis=0)


def _pick_rows(sel, table, k):
    out = jnp.zeros_like(sel)
    for i in range(k):
        out = out + jnp.where(sel == i, table[i:i + 1, :], 0)
    return out


def _topk_kernel(sc_ref, idx_ref, gate_ref):
    kk = PEER_TOPK
    half = kk // 2
    tk = sc_ref.shape[1]
    row16 = lax.broadcasted_iota(I32, (kk, tk), 0)
    row8 = lax.broadcasted_iota(I32, (half, tk), 0)
    code = jnp.concatenate([row16] + [kk * i + row8 for i in range(1, half)] + [kk * (half + row8)], axis=0)
    ids, gates = [], []
    for hd in range(PEER_HEADS):
        r1 = hd * N_KEYS
        r2 = (PEER_HEADS + hd) * N_KEYS
        v1, i1 = _topk_rows(sc_ref[r1:r1 + N_KEYS, :], kk)
        v2, i2 = _topk_rows(sc_ref[r2:r2 + N_KEYS, :], kk)
        cand = jnp.concatenate([v1[0:1] + v2] + [v1[i:i + 1] + v2[0:half] for i in range(1, half)]
                               + [v1[half:kk] + v2[0:1]], axis=0)
        sc, ci = _topk_rows(cand, kk, code)
        e1 = _pick_rows(ci >> 4, i1, kk)
        e2 = _pick_rows(ci & (kk - 1), i2, kk)
        ids.append((e1 * N_KEYS + e2) * ROW_TILES)
        e = jnp.exp(sc - jnp.max(sc, axis=0, keepdims=True))
        gates.append(e / jnp.sum(e, axis=0, keepdims=True))
    idx_ref[...] = jnp.concatenate(ids, axis=0)
    gate_ref[...] = jnp.concatenate(gates, axis=0).T


def _topk(scores):
    n_rows, t = scores.shape
    tk = TK_TOPK
    out = pl.BlockSpec((tk, PEER_PICKS), lambda i: (i, 0))
    return pl.pallas_call(
        _topk_kernel,
        grid=(t // tk,),
        in_specs=[pl.BlockSpec((n_rows, tk), lambda i: (0, i))],
        out_specs=[pl.BlockSpec((PEER_PICKS, tk), lambda i: (0, i)), out],
        out_shape=[jax.ShapeDtypeStruct((PEER_PICKS, t), I32), jax.ShapeDtypeStruct((t, PEER_PICKS), F32)],
        compiler_params=_params(32),
        name="topk",
    )(scores)


def _pack_table(tab):
    e = tab.shape[0]
    tb = tab.astype(BF16).reshape(e, ROW_TILES, 2, LANES)
    return lax.bitcast_convert_type(jnp.swapaxes(tb, 2, 3), I32).reshape(e * ROW_TILES, LANES)


def _peer_constants():
    n = jnp.arange(PICK_COLS)
    mask = (n[None, :] % SUBLANES == jnp.arange(SUBLANES)[:, None]).astype(F32)
    sel = (n[:, None] // SUBLANES == jnp.arange(PEER_PICKS)[None, :]).astype(BF16)
    return mask, sel


def _split(x):
    hi = x.astype(BF16)
    return hi, (x - hi.astype(F32)).astype(BF16)


def _stack_split(x):
    hi = x.astype(BF16).astype(F32)
    return jnp.concatenate([hi, x - hi], axis=0).astype(BF16)


def _fold(y):
    return y[0:SUBLANES] + y[SUBLANES:2 * SUBLANES]


def _tile_ids(idx):
    picks, t = idx.shape
    z = idx.reshape(picks, t // TB_PEER, TB_PEER // LANES, LANES)
    return jnp.transpose(z, (1, 0, 2, 3)).reshape(-1, LANES)


def _idx_copy(idx_hbm, idx_s, sems, tile, slot):
    return pltpu.make_async_copy(idx_hbm.at[pl.ds(tile * IDX_ROWS, IDX_ROWS)],
                                 idx_s.at[pl.ds(slot * IDX_ROWS, IDX_ROWS)], sems.at[slot])


def _fetch_tile(tab_hbm, tab, tab_sem, idx_hbm, idx_s, idx_sems):
    i = pl.program_id(0)
    slot = i % 2

    @pl.when(i == 0)
    def _():
        cp = pltpu.make_async_copy(tab_hbm, tab, tab_sem)
        cp.start()
        _idx_copy(idx_hbm, idx_s, idx_sems, 0, 0).start()
        cp.wait()

    @pl.when(i + 1 < pl.num_programs(0))
    def _():
        _idx_copy(idx_hbm, idx_s, idx_sems, i + 1, 1 - slot).start()

    _idx_copy(idx_hbm, idx_s, idx_sems, i, slot).wait()
    return slot * IDX_ROWS


def _gather_rows(tab, idx_s, row0, t, g_ref):
    q = TB_PEER // LANES
    row = row0 + t // LANES
    lane = t % LANES
    for k in range(PEER_PICKS):
        ids_k = idx_s.at[pl.ds(k * q, IDX_ROWS + q)]
        e4 = pl.multiple_of(ids_k[row, lane], ROW_TILES)
        g_ref[k * ROW_TILES:(k + 1) * ROW_TILES, :] = tab[pl.ds(e4, ROW_TILES), :]


def _token_groups(tab, idx_s, row0, bufs, compute):
    n = len(bufs)
    _gather_rows(tab, idx_s, row0, 0, bufs[0])

    def group(p, _):
        for j in range(n):
            t = p * n + j
            compute(t, bufs[j])
            _gather_rows(tab, idx_s, row0, jnp.minimum(t + 1, TB_PEER - 1), bufs[(j + 1) % n])
        return 0

    lax.fori_loop(0, TB_PEER // n, group, 0)


def _peer_u_kernel(x_ref, gate_ref, mask_ref, sel_ref, idx_hbm, tab_hbm, w_ref,
                   tab, idx_s, r_s, tab_sem, idx_sems, *bufs):
    row0 = _fetch_tile(tab_hbm, tab, tab_sem, idx_hbm, idx_s, idx_sems)
    mask = mask_ref[...]

    def compute(t, g_ref):
        x2 = _stack_split(x_ref[t])
        g = pltpu.bitcast(g_ref[...], BF16)
        out = _fold(_dot_nt(x2, g))
        r_s[pl.ds(t, 1), :] = jnp.sum(out * mask, axis=0, keepdims=True)

    _token_groups(tab, idx_s, row0, bufs, compute)
    rh, rl = _split(r_s[...])
    h = _dot(rh, sel_ref[...]) + _dot(rl, sel_ref[...])
    w_ref[...] = gate_ref[...] * _gelu(h)


def _peer_scratch(tab):
    return ([pltpu.VMEM(tab.shape, I32), pltpu.SMEM((2 * IDX_ROWS, LANES), I32),
             pltpu.VMEM((TB_PEER, PICK_COLS), F32),
             pltpu.SemaphoreType.DMA, pltpu.SemaphoreType.DMA((2,))]
            + [pltpu.VMEM((G_ROWS, LANES), I32)] * PEER_UNROLL)


def _peer_u(x3, gate, idx, tab_u, mask, sel):
    t = x3.shape[0]
    tb = TB_PEER
    any_spec = pl.BlockSpec(memory_space=pl.ANY)
    row = pl.BlockSpec((tb, PEER_PICKS), lambda i: (i, 0))
    return pl.pallas_call(
        _peer_u_kernel,
        grid=(t // tb,),
        in_specs=[pl.BlockSpec((tb, SUBLANES, LANES), lambda i: (i, 0, 0)), row, _full(mask.shape),
                  _full(sel.shape), any_spec, any_spec],
        out_specs=row,
        out_shape=jax.ShapeDtypeStruct((t, PEER_PICKS), F32),
        scratch_shapes=_peer_scratch(tab_u),
        compiler_params=_params(48),
        name="peer_u",
    )(x3, gate, mask, sel, idx, tab_u)


def _peer_v_kernel(w_ref, mask_ref, selt_ref, idx_hbm, tab_hbm, y_ref,
                   tab, idx_s, wx_s, tab_sem, idx_sems, *bufs):
    row0 = _fetch_tile(tab_hbm, tab, tab_sem, idx_hbm, idx_s, idx_sems)
    mask = mask_ref[...]
    wh, wl = _split(w_ref[...])
    wx_s[...] = _dot(wh, selt_ref[...]) + _dot(wl, selt_ref[...])

    def compute(t, g_ref):
        a = jnp.broadcast_to(wx_s[pl.ds(t, 1), :], mask.shape) * mask
        g = pltpu.bitcast(g_ref[...], BF16)
        y_ref[t] = _fold(_dot(_stack_split(a), g))

    _token_groups(tab, idx_s, row0, bufs, compute)


def _peer_v(w, idx, tab_v, mask, selt):
    t = w.shape[0]
    tb = TB_PEER
    any_spec = pl.BlockSpec(memory_space=pl.ANY)
    return pl.pallas_call(
        _peer_v_kernel,
        grid=(t // tb,),
        in_specs=[pl.BlockSpec((tb, PEER_PICKS), lambda i: (i, 0)), _full(mask.shape), _full(selt.shape),
                  any_spec, any_spec],
        out_specs=pl.BlockSpec((tb, SUBLANES, LANES), lambda i: (i, 0, 0)),
        out_shape=jax.ShapeDtypeStruct((t, SUBLANES, LANES), F32),
        scratch_shapes=_peer_scratch(tab_v),
        compiler_params=_params(48),
        name="peer_v",
    )(w, mask, selt, idx, tab_v)


def _final_kernel(x1_ref, y_ref, p_ref, g2_ref, b2_ref, wg_ref, wp_ref, o_ref):
    x2 = _layer_norm(DN_ALPHA * x1_ref[...] + y_ref[...], g2_ref[...], b2_ref[...])
    gate = jax.nn.sigmoid(_dot(x2.astype(BF16), wg_ref[...]))
    o_ref[...] = x2 + gate * _dot(p_ref[...].astype(BF16), wp_ref[...])


def _final(x1, y, p, row0, g2, b2, wg, wp):
    t = p.shape[0]
    tm = TM_FINAL
    blk0 = row0 // tm
    row = lambda w: pl.BlockSpec((tm, w), lambda i: (i, 0))
    tok = lambda w: pl.BlockSpec((tm, w), lambda i: (blk0 + i, 0))
    return pl.pallas_call(
        _final_kernel,
        grid=(t // tm,),
        in_specs=[tok(D_MODEL), tok(D_MODEL), row(PLE_DIM), _full(g2.shape), _full(b2.shape),
                  _full(wg.shape), _full(wp.shape)],
        out_specs=row(D_MODEL),
        out_shape=jax.ShapeDtypeStruct((t, D_MODEL), F32),
        compiler_params=_params(32),
        name="final",
    )(x1, y, p, g2, b2, wg, wp)


def _rope_tables(seq, dec_seq, rows):
    half = HEAD_DIM // 2
    inv = ROPE_THETA ** (-jnp.arange(half, dtype=F32) / half)
    pos = jnp.concatenate([jnp.arange(seq, dtype=jnp.int32),
                           PAST_LEN + jnp.arange(rows, dtype=jnp.int32) % dec_seq])
    ang = pos.astype(F32)[:, None] * inv[None, :]
    cos = jnp.tile(jnp.cos(ang), (1, LANES // half))
    sin = jnp.tile(jnp.sin(ang), (1, LANES // half))
    first = (jnp.arange(LANES) % HEAD_DIM) < half
    return cos, jnp.where(first[None, :], -sin, sin)


def _block_diag(w):
    g, a, b = w.shape
    eye = jnp.eye(g, dtype=w.dtype)
    return jnp.einsum('gab,gh->gahb', w, eye).reshape(g * a, g * b)


def _layer(x_p, x_s, p_p, p_s, win_k, win_v, h0_re, h0_im, batch, seq, dec_batch, dec_seq,
           w_in, attn_sinks, w_attn_proj, lam_re, lam_im, log_dt, b_re, b_im, c_re, c_im, ssm_d,
           w_glu, b_glu, w_ssm_proj, w_out, ln1_g, ln1_b, peer_w_q, keys1, keys2, peer_u, peer_v,
           ln2_g, ln2_b, ple_w_gate, ple_w_proj):
    n_prompt = batch * seq
    row = lambda v: v.reshape(1, -1)

    cos_t, sin_t = _rope_tables(seq, dec_seq, TM_PROJ)
    q, k, v, u, ga, gb = _proj(x_p, x_s, w_in.astype(BF16), cos_t, sin_t, seq // TM_PROJ)

    a_p = _attn_prompt(attn_sinks, q, k, v, batch, seq // WINDOW)
    a_s, new_k_s, new_v_s = _attn_sample(attn_sinks, q, k, v,
                                         win_k.reshape(dec_batch, WINDOW, KV_WIDTH),
                                         win_v.reshape(dec_batch, WINDOW, KV_WIDTH),
                                         n_prompt, dec_batch, dec_seq)

    k_re, k_im, pw_re, pw_im = _ssm_params(lam_re, lam_im, log_dt)
    n = SSM_STATES
    b_full = jnp.concatenate([_block_diag(jnp.swapaxes(b_re, 1, 2)),
                              _block_diag(jnp.swapaxes(b_im, 1, 2))], axis=1).astype(BF16)
    c_full = jnp.concatenate([_block_diag(jnp.swapaxes(c_re, 1, 2)),
                              _block_diag(jnp.swapaxes(-c_im, 1, 2))], axis=0).astype(BF16)
    ssm_w = (b_full, c_full, k_re.reshape(1, n), k_im.reshape(1, n),
             pw_re.reshape(SUBLANES, n), pw_im.reshape(SUBLANES, n), row(ssm_d),
             w_glu.astype(BF16), row(b_glu))
    s_p, hr_p, hi_p = _ssm_prompt(u, ssm_w, batch, seq)
    s_s, hr_s, hi_s = _ssm_sample(u, ssm_w, h0_re.reshape(dec_batch, n), h0_im.reshape(dec_batch, n),
                                  n_prompt, dec_batch, dec_seq)

    x1, scores = _mix(a_p, a_s, s_p, s_s, x_p, x_s, ga, gb, w_attn_proj.astype(BF16), w_ssm_proj.astype(BF16),
                      w_out.astype(BF16), row(ln1_g), row(ln1_b), peer_w_q.astype(BF16),
                      keys1.astype(BF16), keys2.astype(BF16))
    idx, gate = _topk(scores)
    idx = _tile_ids(idx)
    mask, sel = _peer_constants()
    w = _peer_u(x1.reshape(-1, SUBLANES, LANES), gate, idx, _pack_table(peer_u), mask, sel)
    y = _peer_v(w, idx, _pack_table(peer_v), mask, sel.T).reshape(-1, D_MODEL)
    final_w = (row(ln2_g), row(ln2_b), ple_w_gate.astype(BF16), ple_w_proj.astype(BF16))
    out_p = _final(x1, y, p_p, 0, *final_w)
    out_s = _final(x1, y, p_s, n_prompt, *final_w)

    kv5 = lambda z: z.reshape(-1, WINDOW, N_KV_HEADS, HEAD_DIM)
    st = lambda z: z.reshape(-1, N_GROUPS, STATE_DIM)
    last = lambda z: z[:n_prompt].reshape(batch, seq, KV_WIDTH)[:, seq - WINDOW:]
    return (out_p, out_s, kv5(last(k)), kv5(last(v)), st(hr_p), st(hi_p),
            kv5(new_k_s), kv5(new_v_s), st(hr_s), st(hi_s))


def kernel(x_prompt, x_sample, p_prompt, p_sample, state_win_k, state_win_v, state_ssm_re, state_ssm_im,
           w_in, attn_sinks, w_attn_proj, ssm_lambda_re, ssm_lambda_im, ssm_log_dt,
           ssm_b_re, ssm_b_im, ssm_c_re, ssm_c_im, ssm_d, w_glu, b_glu, w_ssm_proj,
           w_out, ln1_g, ln1_b, peer_w_q, peer_keys1, peer_keys2, peer_u, peer_v,
           ln2_g, ln2_b, ple_w_gate, ple_w_proj):
    batch, seq, _ = x_prompt.shape
    dec_batch, dec_seq, _ = x_sample.shape
    depth = w_in.shape[0]
    n_prompt = batch * seq
    weights = (w_in, attn_sinks, w_attn_proj, ssm_lambda_re, ssm_lambda_im, ssm_log_dt,
               ssm_b_re, ssm_b_im, ssm_c_re, ssm_c_im, ssm_d, w_glu, b_glu, w_ssm_proj,
               w_out, ln1_g, ln1_b, peer_w_q, peer_keys1, peer_keys2, peer_u, peer_v,
               ln2_g, ln2_b, ple_w_gate, ple_w_proj)
    x_p = x_prompt.reshape(n_prompt, D_MODEL)
    x_s = x_sample.reshape(-1, D_MODEL)
    per_layer = []
    for i in range(depth):
        x_p, x_s, *states = _layer(x_p, x_s, p_prompt[i].reshape(n_prompt, PLE_DIM),
                                   p_sample[i].reshape(-1, PLE_DIM), state_win_k[i], state_win_v[i],
                                   state_ssm_re[i], state_ssm_im[i], batch, seq, dec_batch, dec_seq,
                                   *(w[i] for w in weights))
        per_layer.append(states)
    stacked = [jnp.stack(z) for z in zip(*per_layer)]
    return (x_p.reshape(batch, seq, D_MODEL), x_s.reshape(dec_batch, dec_seq, D_MODEL), *stacked)
```

```python
import functools

import jax
import jax.numpy as jnp
from jax import lax
from jax.experimental import pallas as pl
from jax.experimental.pallas import tpu as pltpu

F32, BF16, I32 = jnp.float32, jnp.bfloat16, jnp.int32

D_MODEL = 1024
PAST_LEN = 16384
N_HEADS = 8
N_KV_HEADS = 2
HEAD_DIM = 64
WINDOW = 128
ROPE_THETA = 10000.0
ATTN_WIDTH = N_HEADS * HEAD_DIM
KV_WIDTH = N_KV_HEADS * HEAD_DIM
SSM_WIDTH = 512
GROUP_SIZE = 16
N_GROUPS = SSM_WIDTH // GROUP_SIZE
STATE_DIM = 64
SSM_STATES = N_GROUPS * STATE_DIM
PEER_HEADS = 8
N_KEYS = 128
KEY_DIM = 128
PEER_TOPK = 16
PEER_PICKS = PEER_HEADS * PEER_TOPK
PLE_DIM = 256
DN_ALPHA = 2.0 ** 0.25
LN_EPS = 1e-5

C_Q, C_K, C_V, C_U = 0, ATTN_WIDTH, ATTN_WIDTH + KV_WIDTH, ATTN_WIDTH + 2 * KV_WIDTH
C_GA = C_U + SSM_WIDTH
C_GB = C_GA + D_MODEL
IN_WIDTH = C_GB + D_MODEL

LANES = 128
SUBLANES = 8
ROW_TILES = D_MODEL // (2 * LANES)
G_ROWS = PEER_PICKS * ROW_TILES
PICK_COLS = PEER_PICKS * SUBLANES

TM_PROJ = 512
TM_MIX = 256
TM_FINAL = 256
T_SSM = 256
TK_TOPK = 128
TB_PEER = 256
PEER_UNROLL = 32
IDX_ROWS = PEER_PICKS * (TB_PEER // LANES)
MIB = 1024 * 1024


def _params(vmem_mib, n_axes=1):
    return pltpu.CompilerParams(dimension_semantics=("arbitrary",) * n_axes,
                                vmem_limit_bytes=vmem_mib * MIB)


def _full(shape):
    return pl.BlockSpec(shape, lambda *_: (0,) * len(shape))


def _two_part(tm, width, n_first):
    return [pl.BlockSpec((tm, width), lambda i: (jnp.minimum(i, n_first - 1), 0)),
            pl.BlockSpec((tm, width), lambda i: (jnp.maximum(i - n_first, 0), 0))]


def _part(n_first, first_ref, second_ref):
    return jnp.where(pl.program_id(0) < n_first, first_ref[...], second_ref[...])


def _dot(a, b):
    return jnp.dot(a, b, preferred_element_type=F32)


def _dot_nt(a, b):
    return lax.dot_general(a, b, (((1,), (1,)), ((), ())), preferred_element_type=F32)


def _gelu(x):
    return 0.5 * x * (1.0 + lax.erf(x * (2.0 ** -0.5)))


def _layer_norm(x, g, b):
    mu = jnp.mean(x, -1, keepdims=True)
    xc = x - mu
    var = jnp.mean(xc * xc, -1, keepdims=True)
    return xc * lax.rsqrt(var + LN_EPS) * g + b


def _proj_kernel(n_first, xp_ref, xs_ref, w_ref, cos_ref, sin_ref, q_ref, k_ref, v_ref, u_ref, ga_ref, gb_ref):
    xb = _part(n_first, xp_ref, xs_ref).astype(BF16)
    cos = cos_ref[...]
    sin = sin_ref[...]
    lane = lax.broadcasted_iota(I32, cos.shape, 1)
    first = (lane & (HEAD_DIM - 1)) < HEAD_DIM // 2

    def rope(z):
        up = pltpu.roll(z, LANES - HEAD_DIM // 2, axis=1)
        dn = pltpu.roll(z, HEAD_DIM // 2, axis=1)
        return z * cos + jnp.where(first, up, dn) * sin

    for c in range(ATTN_WIDTH // LANES):
        lo = C_Q + c * LANES
        q_ref[:, c * LANES:(c + 1) * LANES] = rope(_dot(xb, w_ref[:, lo:lo + LANES]))
    k_ref[...] = rope(_dot(xb, w_ref[:, C_K:C_V]))
    v_ref[...] = _dot(xb, w_ref[:, C_V:C_U])
    u_ref[...] = _dot(xb, w_ref[:, C_U:C_GA])
    ga_ref[...] = jax.nn.sigmoid(_dot(xb, w_ref[:, C_GA:C_GB]))
    gb_ref[...] = jax.nn.sigmoid(_dot(xb, w_ref[:, C_GB:IN_WIDTH]))


def _proj(x_p, x_s, w_in_b, cos_t, sin_t, tiles_per_seq):
    t = x_p.shape[0] + x_s.shape[0]
    tm = TM_PROJ
    n = t // tm
    n_prompt_tiles = x_p.shape[0] // tm

    def pos_map(i):
        return (jnp.where(i < n_prompt_tiles, i % tiles_per_seq, tiles_per_seq), 0)

    row = lambda w: pl.BlockSpec((tm, w), lambda i: (i, 0))
    widths = (ATTN_WIDTH, KV_WIDTH, KV_WIDTH, SSM_WIDTH, D_MODEL, D_MODEL)
    return pl.pallas_call(
        functools.partial(_proj_kernel, n_prompt_tiles),
        grid=(n,),
        in_specs=_two_part(tm, D_MODEL, n_prompt_tiles) + [_full(w_in_b.shape),
                  pl.BlockSpec((tm, LANES), pos_map), pl.BlockSpec((tm, LANES), pos_map)],
        out_specs=[row(w) for w in widths],
        out_shape=[jax.ShapeDtypeStruct((t, w), F32) for w in widths],
        compiler_params=_params(48),
        name="proj",
    )(x_p, x_s, w_in_b, cos_t, sin_t)


def _attn_core(q, kk, vv, mask, sink_ref):
    lane = lax.broadcasted_iota(I32, (1, LANES), 1)
    lo_half = lane < HEAD_DIM
    kb = kk.astype(BF16)
    kb_sw = pltpu.roll(kk, HEAD_DIM, axis=1).astype(BF16)
    vb = vv.astype(BF16)
    outs = []
    for c in range(ATTN_WIDTH // LANES):
        qc = q[:, c * LANES:(c + 1) * LANES]
        g = c // 2
        halves = []
        for a in range(2):
            h = 2 * c + a
            qm = jnp.where(lo_half if a == 0 else jnp.logical_not(lo_half), qc, 0.0).astype(BF16)
            s = _dot_nt(qm, kb if a == g else kb_sw) * (HEAD_DIM ** -0.5)
            s = jnp.where(mask, s, -jnp.inf)
            sink = sink_ref[h]
            m = jnp.maximum(jnp.max(s, -1, keepdims=True), sink)
            e = jnp.exp(s - m)
            denom = jnp.sum(e, -1, keepdims=True) + jnp.exp(sink - m)
            o = _dot((e / denom).astype(BF16), vb)
            halves.append(o if a == g else pltpu.roll(o, HEAD_DIM, axis=1))
        outs.append(jnp.where(lo_half, halves[0], halves[1]))
    return jnp.concatenate(outs, axis=1)


def _attn_prompt_kernel(sink_ref, q_ref, kp_ref, kc_ref, vp_ref, vc_ref, o_ref):
    j = pl.program_id(1)
    kk = jnp.concatenate([kp_ref[...], kc_ref[...]], axis=0)
    vv = jnp.concatenate([vp_ref[...], vc_ref[...]], axis=0)
    qi = lax.broadcasted_iota(I32, (WINDOW, 2 * WINDOW), 0)
    kj = lax.broadcasted_iota(I32, (WINDOW, 2 * WINDOW), 1)
    rel = qi + WINDOW - kj
    mask = (rel >= 0) & (rel < WINDOW) & ((kj >= WINDOW) | (j > 0))
    o_ref[...] = _attn_core(q_ref[...], kk, vv, mask, sink_ref)


def _attn_prompt(sinks, q, k, v, batch, nb):
    cur = lambda w: pl.BlockSpec((WINDOW, w), lambda b, j: (b * nb + j, 0))
    prev = lambda w: pl.BlockSpec((WINDOW, w), lambda b, j: (b * nb + jnp.maximum(j - 1, 0), 0))
    return pl.pallas_call(
        _attn_prompt_kernel,
        grid=(batch, nb),
        in_specs=[pl.BlockSpec(memory_space=pltpu.SMEM), cur(ATTN_WIDTH),
                  prev(KV_WIDTH), cur(KV_WIDTH), prev(KV_WIDTH), cur(KV_WIDTH)],
        out_specs=cur(ATTN_WIDTH),
        out_shape=jax.ShapeDtypeStruct((batch * nb * WINDOW, ATTN_WIDTH), F32),
        compiler_params=_params(32, 2),
        name="attn_prompt",
    )(sinks, q, k, k, v, v)


def _attn_sample_kernel(dec_seq, group, sink_ref, q_ref, kn_ref, vn_ref, kb_ref, vb_ref,
                        o_ref, nk_ref, nv_ref):
    qi = lax.broadcasted_iota(I32, (dec_seq, 2 * WINDOW), 0)
    kj = lax.broadcasted_iota(I32, (dec_seq, 2 * WINDOW), 1)
    rel = qi + WINDOW - kj
    mask = (rel >= 0) & (rel < WINDOW)
    pad = jnp.zeros((WINDOW - dec_seq, KV_WIDTH), F32)
    for b in range(group):
        rows = slice(b * dec_seq, (b + 1) * dec_seq)
        kk = jnp.concatenate([kb_ref[b], kn_ref[rows, :], pad], axis=0)
        vv = jnp.concatenate([vb_ref[b], vn_ref[rows, :], pad], axis=0)
        o_ref[rows, :] = _attn_core(q_ref[rows, :], kk, vv, mask, sink_ref)
        nk_ref[b] = kk[dec_seq:dec_seq + WINDOW]
        nv_ref[b] = vv[dec_seq:dec_seq + WINDOW]


def _attn_sample(sinks, q, k, v, k_buf, v_buf, row0, dec_batch, dec_seq):
    group = 8
    rows = group * dec_seq
    blk0 = row0 // rows
    tok = lambda w: pl.BlockSpec((rows, w), lambda i: (blk0 + i, 0))
    buf = pl.BlockSpec((group, WINDOW, KV_WIDTH), lambda i: (i, 0, 0))
    return pl.pallas_call(
        functools.partial(_attn_sample_kernel, dec_seq, group),
        grid=(dec_batch // group,),
        in_specs=[pl.BlockSpec(memory_space=pltpu.SMEM), tok(ATTN_WIDTH), tok(KV_WIDTH),
                  tok(KV_WIDTH), buf, buf],
        out_specs=[pl.BlockSpec((rows, ATTN_WIDTH), lambda i: (i, 0)), buf, buf],
        out_shape=[jax.ShapeDtypeStruct((dec_batch * dec_seq, ATTN_WIDTH), F32),
                   jax.ShapeDtypeStruct((dec_batch, WINDOW, KV_WIDTH), F32),
                   jax.ShapeDtypeStruct((dec_batch, WINDOW, KV_WIDTH), F32)],
        compiler_params=_params(32),
        name="attn_sample",
    )(sinks, q, k, v, k_buf, v_buf)


def _ssm_param_kernel(lr_ref, li_ref, ldt_ref, kre_ref, kim_ref, pwr_ref, pwi_ref):
    dt = jnp.exp(ldt_ref[...])
    lr = lr_ref[...]
    li = li_ref[...]
    mag = jnp.exp(lr * dt)
    a_re = mag * jnp.cos(li * dt)
    a_im = mag * jnp.sin(li * dt)
    nr, ni = a_re - 1.0, a_im
    den = lr * lr + li * li
    kre_ref[...] = (nr * lr + ni * li) / den
    kim_ref[...] = (ni * lr - nr * li) / den
    pr, pi = a_re, a_im
    for t in range(SUBLANES):
        pwr_ref[t] = pr
        pwi_ref[t] = pi
        pr, pi = pr * a_re - pi * a_im, pr * a_im + pi * a_re


def _ssm_params(lam_re, lam_im, log_dt):
    gp = (N_GROUPS, STATE_DIM)
    pw = (SUBLANES, N_GROUPS, STATE_DIM)
    return pl.pallas_call(
        _ssm_param_kernel,
        out_shape=[jax.ShapeDtypeStruct(gp, F32), jax.ShapeDtypeStruct(gp, F32),
                   jax.ShapeDtypeStruct(pw, F32), jax.ShapeDtypeStruct(pw, F32)],
        name="ssm_params",
    )(lam_re, lam_im, log_dt.reshape(N_GROUPS, 1))


def _ssm_kernel(sample, u_ref, bf_ref, cf_ref, kre_ref, kim_ref, pwr_ref, pwi_ref, d_ref,
                wglu_ref, bglu_ref, *rest):
    if sample:
        h0r_ref, h0i_ref, s_ref, hr_out, hi_out, xs, hs = rest
    else:
        s_ref, hr_out, hi_out, xs, hs, cr_s, ci_s = rest
    n = SSM_STATES
    u = u_ref[...]
    xs[...] = _dot(u.astype(BF16), bf_ref[...])
    row = lax.broadcasted_iota(I32, (SUBLANES, n), 0)
    kre, kim = kre_ref[...], kim_ref[...]
    pr, pi = pwr_ref[...], pwi_ref[...]
    steps = [(s, pwr_ref[s - 1:s, :], pwi_ref[s - 1:s, :]) for s in (1, 2, 4)]

    if not sample:
        @pl.when(pl.program_id(1) == 0)
        def _():
            cr_s[...] = jnp.zeros_like(cr_s)
            ci_s[...] = jnp.zeros_like(ci_s)

    def slab(j, carry):
        off = pl.multiple_of(j * SUBLANES, SUBLANES)
        br = xs[pl.ds(off, SUBLANES), 0:n]
        bi = xs[pl.ds(off, SUBLANES), n:2 * n]
        xr = kre * br - kim * bi
        xi = kre * bi + kim * br
        for s, ar, ai in steps:
            sr = jnp.where(row >= s, pltpu.roll(xr, s, axis=0), 0.0)
            si = jnp.where(row >= s, pltpu.roll(xi, s, axis=0), 0.0)
            xr, xi = xr + ar * sr - ai * si, xi + ar * si + ai * sr
        if sample:
            cr, ci = h0r_ref[pl.ds(j, 1), :], h0i_ref[pl.ds(j, 1), :]
        else:
            cr, ci = carry
        hr = xr + pr * cr - pi * ci
        hi = xi + pr * ci + pi * cr
        hs[pl.ds(off, SUBLANES), 0:n] = hr
        hs[pl.ds(off, SUBLANES), n:2 * n] = hi
        nr, ni = hr[SUBLANES - 1:SUBLANES, :], hi[SUBLANES - 1:SUBLANES, :]
        if sample:
            hr_out[pl.ds(j, 1), :] = nr
            hi_out[pl.ds(j, 1), :] = ni
            return carry
        return nr, ni

    n_slabs = u.shape[0] // SUBLANES
    if sample:
        lax.fori_loop(0, n_slabs, slab, 0)
    else:
        cr, ci = lax.fori_loop(0, n_slabs, slab, (cr_s[...], ci_s[...]))
        cr_s[...] = cr
        ci_s[...] = ci

        @pl.when(pl.program_id(1) == pl.num_programs(1) - 1)
        def _():
            hr_out[0] = cr
            hi_out[0] = ci

    y = _dot(hs[...].astype(BF16), cf_ref[...]) + d_ref[...] * u
    y = _gelu(y)
    z = _dot(y.astype(BF16), wglu_ref[...]) + bglu_ref[...]
    s_ref[...] = y * jax.nn.sigmoid(z)


def _ssm_weight_specs(n_axes):
    n = SSM_STATES
    shapes = [(SSM_WIDTH, 2 * n), (2 * n, SSM_WIDTH), (1, n), (1, n), (SUBLANES, n), (SUBLANES, n),
              (1, SSM_WIDTH), (SSM_WIDTH, SSM_WIDTH), (1, SSM_WIDTH)]
    return [_full(s) for s in shapes]


def _ssm_prompt(u, weights, batch, seq):
    tt = T_SSM
    nt = seq // tt
    n = SSM_STATES
    tok = pl.BlockSpec((tt, SSM_WIDTH), lambda b, t: (b * nt + t, 0))
    state = pl.BlockSpec((1, 1, n), lambda b, t: (b, 0, 0))
    return pl.pallas_call(
        functools.partial(_ssm_kernel, False),
        grid=(batch, nt),
        in_specs=[tok] + _ssm_weight_specs(2),
        out_specs=[tok, state, state],
        out_shape=[jax.ShapeDtypeStruct((batch * seq, SSM_WIDTH), F32),
                   jax.ShapeDtypeStruct((batch, 1, n), F32), jax.ShapeDtypeStruct((batch, 1, n), F32)],
        scratch_shapes=[pltpu.VMEM((tt, 2 * n), F32), pltpu.VMEM((tt, 2 * n), F32),
                        pltpu.VMEM((1, n), F32), pltpu.VMEM((1, n), F32)],
        compiler_params=_params(48, 2),
        name="ssm_prompt",
    )(u, *weights)


def _ssm_sample(u, weights, h0r, h0i, row0, dec_batch, dec_seq):
    assert dec_seq == SUBLANES
    tt = T_SSM
    nb = tt // dec_seq
    n = SSM_STATES
    blk0 = row0 // tt
    state = pl.BlockSpec((nb, n), lambda i: (i, 0))
    return pl.pallas_call(
        functools.partial(_ssm_kernel, True),
        grid=(dec_batch // nb,),
        in_specs=[pl.BlockSpec((tt, SSM_WIDTH), lambda i: (blk0 + i, 0))] + _ssm_weight_specs(1)
                 + [state, state],
        out_specs=[pl.BlockSpec((tt, SSM_WIDTH), lambda i: (i, 0)), state, state],
        out_shape=[jax.ShapeDtypeStruct((dec_batch * dec_seq, SSM_WIDTH), F32),
                   jax.ShapeDtypeStruct((dec_batch, n), F32), jax.ShapeDtypeStruct((dec_batch, n), F32)],
        scratch_shapes=[pltpu.VMEM((tt, 2 * n), F32), pltpu.VMEM((tt, 2 * n), F32)],
        compiler_params=_params(48),
        name="ssm_sample",
    )(u, *weights, h0r, h0i)


def _mix_kernel(n_first, ap_ref, as_ref, sp_ref, ss_ref, xp_ref, xs_ref, ga_ref, gb_ref,
                wap_ref, wsp_ref, wout_ref, g1_ref, b1_ref, wq_ref, k1_ref, k2_ref, x1_ref, sc_ref):
    a = _part(n_first, ap_ref, as_ref)
    s = _part(n_first, sp_ref, ss_ref)
    mix = (ga_ref[...] * _dot(a.astype(BF16), wap_ref[...])
           + gb_ref[...] * _dot(s.astype(BF16), wsp_ref[...]))
    h = DN_ALPHA * _part(n_first, xp_ref, xs_ref) + _dot(mix.astype(BF16), wout_ref[...])
    x1 = _layer_norm(h, g1_ref[...], b1_ref[...])
    x1_ref[...] = x1
    x1b = x1.astype(BF16)
    for hd in range(PEER_HEADS):
        for half, keys_ref in enumerate((k1_ref, k2_ref)):
            lo = (hd * 2 + half) * KEY_DIM
            qh = _dot(x1b, wq_ref[:, lo:lo + KEY_DIM]).astype(BF16)
            r0 = (half * PEER_HEADS + hd) * N_KEYS
            sc_ref[r0:r0 + N_KEYS, :] = _dot_nt(keys_ref[hd], qh)


def _mix(a_p, a_s, s_p, s_s, x_p, x_s, ga, gb, wap, wsp, wout, g1, b1, wq, k1, k2):
    t = x_p.shape[0] + x_s.shape[0]
    tm = TM_MIX
    n_first = x_p.shape[0] // tm
    row = lambda w: pl.BlockSpec((tm, w), lambda i: (i, 0))
    n_rows = 2 * PEER_HEADS * N_KEYS
    return pl.pallas_call(
        functools.partial(_mix_kernel, n_first),
        grid=(t // tm,),
        in_specs=_two_part(tm, ATTN_WIDTH, n_first) + _two_part(tm, SSM_WIDTH, n_first)
                 + _two_part(tm, D_MODEL, n_first) + [row(D_MODEL), row(D_MODEL),
                  _full(wap.shape), _full(wsp.shape), _full(wout.shape), _full(g1.shape),
                  _full(b1.shape), _full(wq.shape), _full(k1.shape), _full(k2.shape)],
        out_specs=[row(D_MODEL), pl.BlockSpec((n_rows, tm), lambda i: (0, i))],
        out_shape=[jax.ShapeDtypeStruct((t, D_MODEL), F32), jax.ShapeDtypeStruct((n_rows, t), F32)],
        compiler_params=_params(48),
        name="mix",
    )(a_p, a_s, s_p, s_s, x_p, x_s, ga, gb, wap, wsp, wout, g1, b1, wq, k1, k2)


def _topk_rows(s, k, code=None):
    if code is None:
        code = lax.broadcasted_iota(I32, s.shape, 0)
    big = jnp.iinfo(jnp.int32).max
    vals, ids = [], []
    for _ in range(k):
        m = jnp.max(s, axis=0, keepdims=True)
        am = jnp.min(jnp.where(s == m, code, big), axis=0, keepdims=True)
        vals.append(m)
        ids.append(am)
        s = jnp.where(code == am, -jnp.inf, s)
    return jnp.concatenate(vals, axis=0), jnp.concatenate(ids, axis=0)


def _pick_rows(sel, table, k):
    out = jnp.zeros_like(sel)
    for i in range(k):
        out = out + jnp.where(sel == i, table[i:i + 1, :], 0)
    return out


def _topk_kernel(sc_ref, idx_ref, gate_ref):
    kk = PEER_TOPK
    half = kk // 2
    tk = sc_ref.shape[1]
    row16 = lax.broadcasted_iota(I32, (kk, tk), 0)
    row8 = lax.broadcasted_iota(I32, (half, tk), 0)
    code = jnp.concatenate([row16] + [kk * i + row8 for i in range(1, half)] + [kk * (half + row8)], axis=0)
    ids, gates = [], []
    for hd in range(PEER_HEADS):
        r1 = hd * N_KEYS
        r2 = (PEER_HEADS + hd) * N_KEYS
        v1, i1 = _topk_rows(sc_ref[r1:r1 + N_KEYS, :], kk)
        v2, i2 = _topk_rows(sc_ref[r2:r2 + N_KEYS, :], kk)
        cand = jnp.concatenate([v1[0:1] + v2] + [v1[i:i + 1] + v2[0:half] for i in range(1, half)]
                               + [v1[half:kk] + v2[0:1]], axis=0)
        sc, ci = _topk_rows(cand, kk, code)
        e1 = _pick_rows(ci >> 4, i1, kk)
        e2 = _pick_rows(ci & (kk - 1), i2, kk)
        ids.append((e1 * N_KEYS + e2) * ROW_TILES)
        e = jnp.exp(sc - jnp.max(sc, axis=0, keepdims=True))
        gates.append(e / jnp.sum(e, axis=0, keepdims=True))
    idx_ref[...] = jnp.concatenate(ids, axis=0)
    gate_ref[...] = jnp.concatenate(gates, axis=0).T


def _topk(scores):
    n_rows, t = scores.shape
    tk = TK_TOPK
    out = pl.BlockSpec((tk, PEER_PICKS), lambda i: (i, 0))
    return pl.pallas_call(
        _topk_kernel,
        grid=(t // tk,),
        in_specs=[pl.BlockSpec((n_rows, tk), lambda i: (0, i))],
        out_specs=[pl.BlockSpec((PEER_PICKS, tk), lambda i: (0, i)), out],
        out_shape=[jax.ShapeDtypeStruct((PEER_PICKS, t), I32), jax.ShapeDtypeStruct((t, PEER_PICKS), F32)],
        compiler_params=_params(32),
        name="topk",
    )(scores)


def _pack_table(tab):
    e = tab.shape[0]
    tb = tab.astype(BF16).reshape(e, ROW_TILES, 2, LANES)
    return lax.bitcast_convert_type(jnp.swapaxes(tb, 2, 3), I32).reshape(e * ROW_TILES, LANES)


def _peer_constants():
    n = jnp.arange(PICK_COLS)
    mask = (n[None, :] % SUBLANES == jnp.arange(SUBLANES)[:, None]).astype(F32)
    sel = (n[:, None] // SUBLANES == jnp.arange(PEER_PICKS)[None, :]).astype(BF16)
    return mask, sel


def _split(x):
    hi = x.astype(BF16)
    return hi, (x - hi.astype(F32)).astype(BF16)


def _stack_split(x):
    hi = x.astype(BF16).astype(F32)
    return jnp.concatenate([hi, x - hi], axis=0).astype(BF16)


def _fold(y):
    return y[0:SUBLANES] + y[SUBLANES:2 * SUBLANES]


def _tile_ids(idx):
    picks, t = idx.shape
    z = idx.reshape(picks, t // TB_PEER, TB_PEER // LANES, LANES)
    return jnp.transpose(z, (1, 0, 2, 3)).reshape(-1, LANES)


def _idx_copy(idx_hbm, idx_s, sems, tile, slot):
    return pltpu.make_async_copy(idx_hbm.at[pl.ds(tile * IDX_ROWS, IDX_ROWS)],
                                 idx_s.at[pl.ds(slot * IDX_ROWS, IDX_ROWS)], sems.at[slot])


def _fetch_tile(tab_hbm, tab, tab_sem, idx_hbm, idx_s, idx_sems):
    i = pl.program_id(0)
    slot = i % 2

    @pl.when(i == 0)
    def _():
        cp = pltpu.make_async_copy(tab_hbm, tab, tab_sem)
        cp.start()
        _idx_copy(idx_hbm, idx_s, idx_sems, 0, 0).start()
        cp.wait()

    @pl.when(i + 1 < pl.num_programs(0))
    def _():
        _idx_copy(idx_hbm, idx_s, idx_sems, i + 1, 1 - slot).start()

    _idx_copy(idx_hbm, idx_s, idx_sems, i, slot).wait()
    return slot * IDX_ROWS


def _gather_rows(tab, idx_s, row0, t, g_ref):
    q = TB_PEER // LANES
    row = row0 + t // LANES
    lane = t % LANES
    for k in range(PEER_PICKS):
        ids_k = idx_s.at[pl.ds(k * q, IDX_ROWS + q)]
        e4 = pl.multiple_of(ids_k[row, lane], ROW_TILES)
        g_ref[k * ROW_TILES:(k + 1) * ROW_TILES, :] = tab[pl.ds(e4, ROW_TILES), :]


def _token_groups(tab, idx_s, row0, bufs, compute):
    n = len(bufs)
    _gather_rows(tab, idx_s, row0, 0, bufs[0])

    def group(p, _):
        for j in range(n):
            t = p * n + j
            compute(t, bufs[j])
            _gather_rows(tab, idx_s, row0, jnp.minimum(t + 1, TB_PEER - 1), bufs[(j + 1) % n])
        return 0

    lax.fori_loop(0, TB_PEER // n, group, 0)


def _peer_u_kernel(x_ref, gate_ref, mask_ref, sel_ref, idx_hbm, tab_hbm, w_ref,
                   tab, idx_s, r_s, tab_sem, idx_sems, *bufs):
    row0 = _fetch_tile(tab_hbm, tab, tab_sem, idx_hbm, idx_s, idx_sems)
    mask = mask_ref[...]

    def compute(t, g_ref):
        x2 = _stack_split(x_ref[t])
        g = pltpu.bitcast(g_ref[...], BF16)
        out = _fold(_dot_nt(x2, g))
        r_s[pl.ds(t, 1), :] = jnp.sum(out * mask, axis=0, keepdims=True)

    _token_groups(tab, idx_s, row0, bufs, compute)
    rh, rl = _split(r_s[...])
    h = _dot(rh, sel_ref[...]) + _dot(rl, sel_ref[...])
    w_ref[...] = gate_ref[...] * _gelu(h)


def _peer_scratch(tab):
    return ([pltpu.VMEM(tab.shape, I32), pltpu.SMEM((2 * IDX_ROWS, LANES), I32),
             pltpu.VMEM((TB_PEER, PICK_COLS), F32),
             pltpu.SemaphoreType.DMA, pltpu.SemaphoreType.DMA((2,))]
            + [pltpu.VMEM((G_ROWS, LANES), I32)] * PEER_UNROLL)


def _peer_u(x3, gate, idx, tab_u, mask, sel):
    t = x3.shape[0]
    tb = TB_PEER
    any_spec = pl.BlockSpec(memory_space=pl.ANY)
    row = pl.BlockSpec((tb, PEER_PICKS), lambda i: (i, 0))
    return pl.pallas_call(
        _peer_u_kernel,
        grid=(t // tb,),
        in_specs=[pl.BlockSpec((tb, SUBLANES, LANES), lambda i: (i, 0, 0)), row, _full(mask.shape),
                  _full(sel.shape), any_spec, any_spec],
        out_specs=row,
        out_shape=jax.ShapeDtypeStruct((t, PEER_PICKS), F32),
        scratch_shapes=_peer_scratch(tab_u),
        compiler_params=_params(48),
        name="peer_u",
    )(x3, gate, mask, sel, idx, tab_u)


def _peer_v_kernel(w_ref, mask_ref, selt_ref, idx_hbm, tab_hbm, y_ref,
                   tab, idx_s, wx_s, tab_sem, idx_sems, *bufs):
    row0 = _fetch_tile(tab_hbm, tab, tab_sem, idx_hbm, idx_s, idx_sems)
    mask = mask_ref[...]
    wh, wl = _split(w_ref[...])
    wx_s[...] = _dot(wh, selt_ref[...]) + _dot(wl, selt_ref[...])

    def compute(t, g_ref):
        a = jnp.broadcast_to(wx_s[pl.ds(t, 1), :], mask.shape) * mask
        g = pltpu.bitcast(g_ref[...], BF16)
        y_ref[t] = _fold(_dot(_stack_split(a), g))

    _token_groups(tab, idx_s, row0, bufs, compute)


def _peer_v(w, idx, tab_v, mask, selt):
    t = w.shape[0]
    tb = TB_PEER
    any_spec = pl.BlockSpec(memory_space=pl.ANY)
    return pl.pallas_call(
        _peer_v_kernel,
        grid=(t // tb,),
        in_specs=[pl.BlockSpec((tb, PEER_PICKS), lambda i: (i, 0)), _full(mask.shape), _full(selt.shape),
                  any_spec, any_spec],
        out_specs=pl.BlockSpec((tb, SUBLANES, LANES), lambda i: (i, 0, 0)),
        out_shape=jax.ShapeDtypeStruct((t, SUBLANES, LANES), F32),
        scratch_shapes=_peer_scratch(tab_v),
        compiler_params=_params(48),
        name="peer_v",
    )(w, mask, selt, idx, tab_v)


def _final_kernel(x1_ref, y_ref, p_ref, g2_ref, b2_ref, wg_ref, wp_ref, o_ref):
    x2 = _layer_norm(DN_ALPHA * x1_ref[...] + y_ref[...], g2_ref[...], b2_ref[...])
    gate = jax.nn.sigmoid(_dot(x2.astype(BF16), wg_ref[...]))
    o_ref[...] = x2 + gate * _dot(p_ref[...].astype(BF16), wp_ref[...])


def _final(x1, y, p, row0, g2, b2, wg, wp):
    t = p.shape[0]
    tm = TM_FINAL
    blk0 = row0 // tm
    row = lambda w: pl.BlockSpec((tm, w), lambda i: (i, 0))
    tok = lambda w: pl.BlockSpec((tm, w), lambda i: (blk0 + i, 0))
    return pl.pallas_call(
        _final_kernel,
        grid=(t // tm,),
        in_specs=[tok(D_MODEL), tok(D_MODEL), row(PLE_DIM), _full(g2.shape), _full(b2.shape),
                  _full(wg.shape), _full(wp.shape)],
        out_specs=row(D_MODEL),
        out_shape=jax.ShapeDtypeStruct((t, D_MODEL), F32),
        compiler_params=_params(32),
        name="final",
    )(x1, y, p, g2, b2, wg, wp)


def _rope_tables(seq, dec_seq, rows):
    half = HEAD_DIM // 2
    inv = ROPE_THETA ** (-jnp.arange(half, dtype=F32) / half)
    pos = jnp.concatenate([jnp.arange(seq, dtype=jnp.int32),
                           PAST_LEN + jnp.arange(rows, dtype=jnp.int32) % dec_seq])
    ang = pos.astype(F32)[:, None] * inv[None, :]
    cos = jnp.tile(jnp.cos(ang), (1, LANES // half))
    sin = jnp.tile(jnp.sin(ang), (1, LANES // half))
    first = (jnp.arange(LANES) % HEAD_DIM) < half
    return cos, jnp.where(first[None, :], -sin, sin)


def _block_diag(w):
    g, a, b = w.shape
    eye = jnp.eye(g, dtype=w.dtype)
    return jnp.einsum('gab,gh->gahb', w, eye).reshape(g * a, g * b)


def _layer(x_p, x_s, p_p, p_s, win_k, win_v, h0_re, h0_im, batch, seq, dec_batch, dec_seq,
           w_in, attn_sinks, w_attn_proj, lam_re, lam_im, log_dt, b_re, b_im, c_re, c_im, ssm_d,
           w_glu, b_glu, w_ssm_proj, w_out, ln1_g, ln1_b, peer_w_q, keys1, keys2, peer_u, peer_v,
           ln2_g, ln2_b, ple_w_gate, ple_w_proj):
    n_prompt = batch * seq
    row = lambda v: v.reshape(1, -1)

    cos_t, sin_t = _rope_tables(seq, dec_seq, TM_PROJ)
    q, k, v, u, ga, gb = _proj(x_p, x_s, w_in.astype(BF16), cos_t, sin_t, seq // TM_PROJ)

    a_p = _attn_prompt(attn_sinks, q, k, v, batch, seq // WINDOW)
    a_s, new_k_s, new_v_s = _attn_sample(attn_sinks, q, k, v,
                                         win_k.reshape(dec_batch, WINDOW, KV_WIDTH),
                                         win_v.reshape(dec_batch, WINDOW, KV_WIDTH),
                                         n_prompt, dec_batch, dec_seq)

    k_re, k_im, pw_re, pw_im = _ssm_params(lam_re, lam_im, log_dt)
    n = SSM_STATES
    b_full = jnp.concatenate([_block_diag(jnp.swapaxes(b_re, 1, 2)),
                              _block_diag(jnp.swapaxes(b_im, 1, 2))], axis=1).astype(BF16)
    c_full = jnp.concatenate([_block_diag(jnp.swapaxes(c_re, 1, 2)),
                              _block_diag(jnp.swapaxes(-c_im, 1, 2))], axis=0).astype(BF16)
    ssm_w = (b_full, c_full, k_re.reshape(1, n), k_im.reshape(1, n),
             pw_re.reshape(SUBLANES, n), pw_im.reshape(SUBLANES, n), row(ssm_d),
             w_glu.astype(BF16), row(b_glu))
    s_p, hr_p, hi_p = _ssm_prompt(u, ssm_w, batch, seq)
    s_s, hr_s, hi_s = _ssm_sample(u, ssm_w, h0_re.reshape(dec_batch, n), h0_im.reshape(dec_batch, n),
                                  n_prompt, dec_batch, dec_seq)

    x1, scores = _mix(a_p, a_s, s_p, s_s, x_p, x_s, ga, gb, w_attn_proj.astype(BF16), w_ssm_proj.astype(BF16),
                      w_out.astype(BF16), row(ln1_g), row(ln1_b), peer_w_q.astype(BF16),
                      keys1.astype(BF16), keys2.astype(BF16))
    idx, gate = _topk(scores)
    idx = _tile_ids(idx)
    mask, sel = _peer_constants()
    w = _peer_u(x1.reshape(-1, SUBLANES, LANES), gate, idx, _pack_table(peer_u), mask, sel)
    y = _peer_v(w, idx, _pack_table(peer_v), mask, sel.T).reshape(-1, D_MODEL)
    final_w = (row(ln2_g), row(ln2_b), ple_w_gate.astype(BF16), ple_w_proj.astype(BF16))
    out_p = _final(x1, y, p_p, 0, *final_w)
    out_s = _final(x1, y, p_s, n_prompt, *final_w)

    kv5 = lambda z: z.reshape(-1, WINDOW, N_KV_HEADS, HEAD_DIM)
    st = lambda z: z.reshape(-1, N_GROUPS, STATE_DIM)
    last = lambda z: z[:n_prompt].reshape(batch, seq, KV_WIDTH)[:, seq - WINDOW:]
    return (out_p, out_s, kv5(last(k)), kv5(last(v)), st(hr_p), st(hi_p),
            kv5(new_k_s), kv5(new_v_s), st(hr_s), st(hi_s))


def kernel(x_prompt, x_sample, p_prompt, p_sample, state_win_k, state_win_v, state_ssm_re, state_ssm_im,
           w_in, attn_sinks, w_attn_proj, ssm_lambda_re, ssm_lambda_im, ssm_log_dt,
           ssm_b_re, ssm_b_im, ssm_c_re, ssm_c_im, ssm_d, w_glu, b_glu, w_ssm_proj,
           w_out, ln1_g, ln1_b, peer_w_q, peer_keys1, peer_keys2, peer_u, peer_v,
           ln2_g, ln2_b, ple_w_gate, ple_w_proj):
    batch, seq, _ = x_prompt.shape
    dec_batch, dec_seq, _ = x_sample.shape
    depth = w_in.shape[0]
    n_prompt = batch * seq
    weights = (w_in, attn_sinks, w_attn_proj, ssm_lambda_re, ssm_lambda_im, ssm_log_dt,
               ssm_b_re, ssm_b_im, ssm_c_re, ssm_c_im, ssm_d, w_glu, b_glu, w_ssm_proj,
               w_out, ln1_g, ln1_b, peer_w_q, peer_keys1, peer_keys2, peer_u, peer_v,
               ln2_g, ln2_b, ple_w_gate, ple_w_proj)
    x_p = x_prompt.reshape(n_prompt, D_MODEL)
    x_s = x_sample.reshape(-1, D_MODEL)
    per_layer = []
    for i in range(depth):
        x_p, x_s, *states = _layer(x_p, x_s, p_prompt[i].reshape(n_prompt, PLE_DIM),
                                   p_sample[i].reshape(-1, PLE_DIM), state_win_k[i], state_win_v[i],
                                   state_ssm_re[i], state_ssm_im[i], batch, seq, dec_batch, dec_seq,
                                   *(w[i] for w in weights))
        per_layer.append(states)
    stacked = [jnp.stack(z) for z in zip(*per_layer)]
    return (x_p.reshape(batch, seq, D_MODEL), x_s.reshape(dec_batch, dec_seq, D_MODEL), *stacked)
```
